```python
import jax, jax.numpy as jnp
from jax import lax
import numpy as np


D_MODEL = 1024
BATCH = 4
SEQ = 8192
DEPTH = 4

HEAD_DIM = 64
ROPE_THETA = 10000.0
EPS = 1e-6
NEG = -1e30
TINY = 1e-30
FORCE_SCORE = 1e9

DIL_PAIRS = ((128, 1), (512, 4), (2048, 16))
N_DIL = 3
A_HEADS_PER_GROUP = 8
A_HEADS = N_DIL * A_HEADS_PER_GROUP
A_BLOCK = 128
A_OUT = A_HEADS_PER_GROUP * HEAD_DIM

B_Q_HEADS = 8
B_KV_HEADS = 2
B_GQA = B_Q_HEADS // B_KV_HEADS
CMP_LEN = 32
CMP_STRIDE = 16
CMP_HIDDEN = 256
SEL_LEN = 64
N_SELECT = 16
WIN = 512
B_BLOCK = 128
B_OUT = B_Q_HEADS * HEAD_DIM

D_FF = 2816
CONV_W = 3

A_QKV = 3 * A_HEADS * HEAD_DIM
B_QD = B_Q_HEADS * HEAD_DIM
B_KV = 3 * 2 * B_KV_HEADS * HEAD_DIM
B_GATE = 3 * B_Q_HEADS
MERGE = 2 * D_MODEL
N_IN = A_QKV + B_QD + B_KV + B_GATE + MERGE
IN_SPLITS = (A_QKV, A_QKV + B_QD, A_QKV + B_QD + B_KV, A_QKV + B_QD + B_KV + B_GATE)

kernel_name = 'hybrid_dilated_nsa_convffn'


def rmsnorm(x, g):
    xf = x.astype(jnp.float32)
    y = xf * lax.rsqrt(jnp.mean(xf * xf, axis=-1, keepdims=True) + EPS)
    return (y * g.astype(jnp.float32)).astype(x.dtype)


def rope_tables(pos):
    half = HEAD_DIM // 2
    inv_freq = ROPE_THETA ** (-jnp.arange(half, dtype=jnp.float32) / half)
    ang = pos.astype(jnp.float32)[:, None] * inv_freq[None, :]
    return jnp.cos(ang), jnp.sin(ang)


def apply_rope(x, cos, sin):
    half = HEAD_DIM // 2
    shape = (cos.shape[0],) + (1,) * (x.ndim - 3) + (half,)
    c = cos.reshape(shape).astype(x.dtype)
    s = sin.reshape(shape).astype(x.dtype)
    x1, x2 = x[..., :half], x[..., half:]
    return jnp.concatenate([x1 * c - x2 * s, x2 * c + x1 * s], axis=-1)


def banded_attention(q, k, v, max_dist, block):
    n, L, hk, g, dh = q.shape
    nb = L // block
    n_prev = -(-max_dist // block)
    qb = q.reshape(n, nb, block, hk, g, dh)

    def band(t):
        tb = t.reshape(n, nb, block, hk, dh)
        tp = jnp.pad(tb, ((0, 0), (n_prev, 0), (0, 0), (0, 0), (0, 0)))
        return jnp.concatenate([tp[:, i:i + nb] for i in range(n_prev + 1)], axis=2)

    kb, vb = band(k), band(v)
    s = jnp.einsum('nbqhgd,nbkhd->nbhgqk', qb, kb, preferred_element_type=jnp.float32) * (dh ** -0.5)
    qpos = jnp.arange(nb)[:, None] * block + jnp.arange(block)[None, :]
    kpos = (jnp.arange(nb)[:, None] - n_prev) * block + jnp.arange((n_prev + 1) * block)[None, :]
    dist = qpos[:, :, None] - kpos[:, None, :]
    valid = (dist >= 0) & (dist <= max_dist) & (kpos[:, None, :] >= 0)
    s = jnp.where(valid[None, :, None, None], s, NEG)
    lse = jax.nn.logsumexp(s, axis=-1)
    p = jnp.exp(s - lse[..., None])
    o = jnp.einsum('nbhgqk,nbkhd->nbqhgd', p.astype(v.dtype), vb)
    return o.reshape(n, L, hk, g, dh), lse.transpose(0, 1, 4, 2, 3).reshape(n, L, hk, g)


def dilated_mixer(q, k, v):
    B, S, _, dh = q.shape
    hpg = A_HEADS_PER_GROUP
    outs, lses = [], []
    for gi, (window, dil) in enumerate(DIL_PAIRS):
        hs = slice(gi * hpg, (gi + 1) * hpg)
        L = S // dil
        Lp = -(-L // A_BLOCK) * A_BLOCK

        def to_sub(t):
            t = t.reshape(B, L, dil, hpg, dh).transpose(0, 2, 1, 3, 4).reshape(B * dil, L, hpg, dh)
            return jnp.pad(t, ((0, 0), (0, Lp - L), (0, 0), (0, 0)))

        o, lse = banded_attention(to_sub(q[:, :, hs])[:, :, :, None], to_sub(k[:, :, hs]),
                                  to_sub(v[:, :, hs]), window // dil, A_BLOCK)
        o = o[:, :L, :, 0].reshape(B, dil, L, hpg, dh).transpose(0, 2, 1, 3, 4).reshape(B, S, hpg, dh)
        lse = lse[:, :L, :, 0].reshape(B, dil, L, hpg).transpose(0, 2, 1, 3).reshape(B, S, hpg)
        outs.append(o)
        lses.append(lse)
    wts = jax.nn.softmax(jnp.stack(lses, axis=0), axis=0)
    o = jnp.sum(wts[..., None] * jnp.stack(outs, axis=0).astype(jnp.float32), axis=0)
    return o.astype(q.dtype).reshape(B, S, A_OUT)


def compress(t, pos_emb, w1, w2):
    B, S, H, dh = t.shape
    r = CMP_LEN // CMP_STRIDE
    nc = S // CMP_STRIDE
    n_cmp = nc - r + 1
    ch = t.reshape(B, nc, CMP_STRIDE, H, dh)
    blk = jnp.concatenate([ch[:, i:i + n_cmp] for i in range(r)], axis=2)
    blk = blk + pos_emb[None, None, :, None, :].astype(t.dtype)
    flat = blk.transpose(0, 1, 3, 2, 4).reshape(B, n_cmp, H, CMP_LEN * dh)
    return jax.nn.gelu(flat @ w1) @ w2


def nsa_mixer(q, k_cmp, v_cmp, k_sel, v_sel, k_win, v_win, gate_logits):
    B, S, H, G, dh = q.shape
    scale = dh ** -0.5
    n_cmp = k_cmp.shape[1]
    n_slc = S // SEL_LEN
    n_sel = min(N_SELECT, n_slc)
    r = CMP_LEN // CMP_STRIDE
    nq = S // B_BLOCK
    cmp_end = jnp.arange(n_cmp) * CMP_STRIDE + CMP_LEN - 1
    blk_ids = jnp.arange(n_slc)
    kb_sel = k_sel.reshape(B, n_slc, SEL_LEN, H, dh).transpose(0, 3, 1, 2, 4)
    vb_sel = v_sel.reshape(B, n_slc, SEL_LEN, H, dh).transpose(0, 3, 1, 2, 4)
    b_idx = jnp.arange(B)[:, None, None, None]
    h_idx = jnp.arange(H)[None, None, :, None]
    q_blocks = q.reshape(B, nq, B_BLOCK, H, G, dh).transpose(1, 0, 2, 3, 4, 5)

    def step(args):
        qb, bi = args
        t = bi * B_BLOCK + jnp.arange(B_BLOCK)
        s = jnp.einsum('bqhgd,bchd->bqhgc', qb, k_cmp, preferred_element_type=jnp.float32) * scale
        valid = (cmp_end[None, :] <= t[:, None])[None, :, None, None, :]
        s_m = jnp.where(valid, s, NEG)
        m = jnp.max(s_m, axis=-1, keepdims=True)
        e = jnp.where(valid, jnp.exp(s_m - m), 0.0)
        p = e / jnp.maximum(jnp.sum(e, axis=-1, keepdims=True), TINY)
        o_cmp = jnp.einsum('bqhgc,bchd->bqhgd', p.astype(v_cmp.dtype), v_cmp)
        imp = jnp.sum(p, axis=3)
        chunk = sum(jnp.pad(imp, ((0, 0), (0, 0), (0, 0), (i, r - 1 - i))) for i in range(r))
        p_slc = chunk.reshape(B, B_BLOCK, H, n_slc, SEL_LEN // CMP_STRIDE).sum(-1)
        cur = t // SEL_LEN
        forced = (blk_ids[None] == 0) | (blk_ids[None] == cur[:, None]) | (blk_ids[None] == cur[:, None] - 1)
        causal_blk = blk_ids[None] * SEL_LEN <= t[:, None]
        score = jnp.where(forced[None, :, None], FORCE_SCORE,
                          jnp.where(causal_blk[None, :, None], p_slc, -1.0))
        _, sel = lax.top_k(score, n_sel)
        kg = kb_sel[b_idx, h_idx, sel].reshape(B, B_BLOCK, H, n_sel * SEL_LEN, dh)
        vg = vb_sel[b_idx, h_idx, sel].reshape(B, B_BLOCK, H, n_sel * SEL_LEN, dh)
        kpos = (sel[..., None] * SEL_LEN + jnp.arange(SEL_LEN)).reshape(B, B_BLOCK, H, n_sel * SEL_LEN)
        s2 = jnp.einsum('bqhgd,bqhkd->bqhgk', qb, kg, preferred_element_type=jnp.float32) * scale
        s2 = jnp.where((kpos <= t[None, :, None, None])[:, :, :, None, :], s2, NEG)
        p2 = jax.nn.softmax(s2, axis=-1)
        o_sel = jnp.einsum('bqhgk,bqhkd->bqhgd', p2.astype(vg.dtype), vg)
        return o_cmp, o_sel

    o_cmp, o_sel = lax.map(step, (q_blocks, jnp.arange(nq)))
    o_cmp = o_cmp.transpose(1, 0, 2, 3, 4, 5).reshape(B, S, H, G, dh)
    o_sel = o_sel.transpose(1, 0, 2, 3, 4, 5).reshape(B, S, H, G, dh)
    o_win, _ = banded_attention(q, k_win, v_win, WIN - 1, B_BLOCK)
    g = jax.nn.sigmoid(gate_logits.reshape(B, S, 3, H, G, 1))
    o = g[:, :, 0] * o_cmp + g[:, :, 1] * o_sel + g[:, :, 2] * o_win
    return o.reshape(B, S, B_OUT)


def causal_dwconv(t, w, b):
    S = t.shape[1]
    tp = jnp.pad(t, ((0, 0), (CONV_W - 1, 0), (0, 0)))
    out = b
    for i in range(CONV_W):
        out = out + w[i] * tp[:, i:i + S]
    return out


def setup_inputs(seed: int = 0) -> dict:
    key = jax.random.key(seed)
    ks = jax.random.split(key, 18)

    def nrm(k, shape, fan):
        return jax.random.normal(k, shape, jnp.float32) * fan ** -0.5

    def gain(k, shape):
        return 1.0 + 0.01 * jax.random.normal(k, shape, jnp.float32)

    return {
        'x': jax.random.normal(ks[0], (BATCH, SEQ, D_MODEL), jnp.float32),
        'norm_mix_g': gain(ks[1], (DEPTH, D_MODEL)),
        'w_in': nrm(ks[2], (DEPTH, D_MODEL, N_IN), D_MODEL),
        'a_q_g': gain(ks[3], (DEPTH, HEAD_DIM)),
        'a_k_g': gain(ks[4], (DEPTH, HEAD_DIM)),
        'b_q_g': gain(ks[5], (DEPTH, HEAD_DIM)),
        'b_k_g': gain(ks[6], (DEPTH, 3, HEAD_DIM)),
        'cmp_pos': 0.02 * jax.random.normal(ks[7], (DEPTH, 2, CMP_LEN, HEAD_DIM), jnp.float32),
        'cmp_w1': nrm(ks[8], (DEPTH, 2, CMP_LEN * HEAD_DIM, CMP_HIDDEN), CMP_LEN * HEAD_DIM),
        'cmp_w2': nrm(ks[9], (DEPTH, 2, CMP_HIDDEN, HEAD_DIM), CMP_HIDDEN),
        'w_proj_a': nrm(ks[10], (DEPTH, A_OUT, D_MODEL), A_OUT),
        'w_proj_b': nrm(ks[11], (DEPTH, B_OUT, D_MODEL), B_OUT),
        'w_out': nrm(ks[12], (DEPTH, D_MODEL, D_MODEL), D_MODEL),
        'norm_ffn_g': gain(ks[13], (DEPTH, D_MODEL)),
        'w_up': nrm(ks[14], (DEPTH, D_MODEL, 2 * D_FF), D_MODEL),
        'conv_w': nrm(ks[15], (DEPTH, CONV_W, D_FF), CONV_W),
        'conv_b': 0.01 * jax.random.normal(ks[16], (DEPTH, D_FF), jnp.float32),
        'w_down': nrm(ks[17], (DEPTH, D_FF, D_MODEL), D_FF),
    }


def reference(x, norm_mix_g, w_in, a_q_g, a_k_g, b_q_g, b_k_g, cmp_pos, cmp_w1, cmp_w2,
              w_proj_a, w_proj_b, w_out, norm_ffn_g, w_up, conv_w, conv_b, w_down):
    B, S, _ = x.shape
    cos, sin = rope_tables(jnp.arange(S))
    n_cmp = S // CMP_STRIDE - CMP_LEN // CMP_STRIDE + 1
    cos_c, sin_c = rope_tables(jnp.arange(n_cmp) * CMP_STRIDE + CMP_LEN - 1)
    for l in range(DEPTH):
        h = rmsnorm(x, norm_mix_g[l])
        proj = h @ w_in[l]
        a_qkv, b_q, b_kv, b_gate, merge = jnp.split(proj, IN_SPLITS, axis=-1)
        a_qkv = a_qkv.reshape(B, S, 3, A_HEADS, HEAD_DIM)
        qa = apply_rope(rmsnorm(a_qkv[:, :, 0], a_q_g[l]), cos, sin)
        ka = apply_rope(rmsnorm(a_qkv[:, :, 1], a_k_g[l]), cos, sin)
        out_a = dilated_mixer(qa, ka, a_qkv[:, :, 2])
        qb = apply_rope(rmsnorm(b_q.reshape(B, S, B_KV_HEADS, B_GQA, HEAD_DIM), b_q_g[l]), cos, sin)
        b_kv = b_kv.reshape(B, S, 3, 2, B_KV_HEADS, HEAD_DIM)
        k_cmp = compress(b_kv[:, :, 0, 0], cmp_pos[l, 0], cmp_w1[l, 0], cmp_w2[l, 0])
        k_cmp = apply_rope(rmsnorm(k_cmp, b_k_g[l, 0]), cos_c, sin_c)
        v_cmp = compress(b_kv[:, :, 0, 1], cmp_pos[l, 1], cmp_w1[l, 1], cmp_w2[l, 1])
        k_sel = apply_rope(rmsnorm(b_kv[:, :, 1, 0], b_k_g[l, 1]), cos, sin)
        k_win = apply_rope(rmsnorm(b_kv[:, :, 2, 0], b_k_g[l, 2]), cos, sin)
        out_b = nsa_mixer(qb, k_cmp, v_cmp, k_sel, b_kv[:, :, 1, 1], k_win, b_kv[:, :, 2, 1], b_gate)
        gate = jax.nn.sigmoid(merge.reshape(B, S, 2, D_MODEL))
        mixed = gate[:, :, 0] * (out_a @ w_proj_a[l]) + gate[:, :, 1] * (out_b @ w_proj_b[l])
        x = x + mixed @ w_out[l]
        h = rmsnorm(x, norm_ffn_g[l])
        g_in, u = jnp.split(h @ w_up[l], 2, axis=-1)
        x = x + (jax.nn.silu(causal_dwconv(g_in, conv_w[l], conv_b[l])) * u) @ w_down[l]
    return x
```

```python
import functools

import numpy as np
import jax
import jax.numpy as jnp
from jax import lax
from jax.experimental import pallas as pl
from jax.experimental.pallas import tpu as pltpu

F32 = jnp.float32
BF16 = jnp.bfloat16
I32 = jnp.int32

D_MODEL = 1024
HEAD_DIM = 64
ROPE_THETA = 10000.0
EPS = 1e-6
NEG = -1e30
TINY = 1e-30
FORCE_SCORE = 1e9

DIL_PAIRS = ((128, 1), (512, 4), (2048, 16))
A_HEADS_PER_GROUP = 8
A_HEADS = 24
A_OUT = 512
A_BAND = 128

B_Q_HEADS = 8
B_KV_HEADS = 2
B_GQA = 4
CMP_LEN = 32
CMP_STRIDE = 16
CMP_HIDDEN = 256
SEL_LEN = 64
SEL_SHIFT = 6
N_SELECT = 16
WIN = 512
B_OUT = 512

D_FF = 2816
CONV_W = 3

A_QKV = 3 * A_HEADS * HEAD_DIM
B_QD = B_Q_HEADS * HEAD_DIM
B_KV = 3 * 2 * B_KV_HEADS * HEAD_DIM
B_GATE = 3 * B_Q_HEADS

LANES = 128
TQ = 128
SEL_TK = 512
VMEM_LIMIT = 48 * 1024 * 1024

_NT = (((1,), (1,)), ((), ()))


def _dot(a, b):
    return jnp.dot(a, b, preferred_element_type=F32)


def _dot_nt(a, b):
    return lax.dot_general(a, b, _NT, preferred_element_type=F32)


def _split_dot(x, w2):
    hi = x.astype(BF16)
    lo = (x - hi.astype(F32)).astype(BF16)
    return _dot(jnp.concatenate([hi, lo], axis=1), w2)


def _params(*sem):
    return pltpu.CompilerParams(dimension_semantics=sem, vmem_limit_bytes=VMEM_LIMIT)


def _rmsnorm_kernel(x_ref, g_ref, h_ref):
    x = x_ref[...]
    y = x * lax.rsqrt(jnp.mean(x * x, axis=-1, keepdims=True) + EPS)
    h_ref[...] = (y * g_ref[...]).astype(h_ref.dtype)


def _rmsnorm(x, g, tm=512):
    t, d = x.shape
    return pl.pallas_call(
        _rmsnorm_kernel,
        grid=(t // tm,),
        in_specs=[pl.BlockSpec((tm, d), lambda i: (i, 0)), pl.BlockSpec((1, d), lambda i: (0, 0))],
        out_specs=pl.BlockSpec((tm, d), lambda i: (i, 0)),
        out_shape=jax.ShapeDtypeStruct((t, d), BF16),
        compiler_params=_params("parallel"),
        name="rmsnorm",
    )(x, g.reshape(1, d))


def _head_norm_rope(y, gain, cos_t, sin_t, seg2):
    ss = _split_dot(y * y, seg2)
    yn = y * lax.rsqrt(ss * (1.0 / HEAD_DIM) + EPS) * gain
    lane = lax.broadcasted_iota(I32, yn.shape, 1)
    partner = jnp.where((lane & 32) == 0, pltpu.roll(yn, 96, 1), pltpu.roll(yn, 32, 1))
    return yn * cos_t + partner * sin_t


def _proj_kernel(h_ref, w_ref, o_ref):
    o_ref[...] = _dot(h_ref[...], w_ref[...]).astype(o_ref.dtype)


def _proj_rope_kernel(h_ref, w_ref, gain_ref, cos_ref, sin_ref, seg_ref, o_ref):
    y = _dot(h_ref[...], w_ref[...])
    cos_t, sin_t, seg2 = cos_ref[...], sin_ref[...], seg_ref[...]
    for c in range(y.shape[1] // LANES):
        sl = slice(c * LANES, (c + 1) * LANES)
        o_ref[:, sl] = _head_norm_rope(y[:, sl], gain_ref[:, sl], cos_t, sin_t, seg2).astype(o_ref.dtype)


def _proj(h, w, out_dtype, tn, tm=512, rope=None, seq=None):
    t, k = h.shape
    n = w.shape[1]
    grid = (n // tn, t // tm)
    h_spec = pl.BlockSpec((tm, k), lambda j, i: (i, 0))
    w_spec = pl.BlockSpec((k, tn), lambda j, i: (0, j))
    o_spec = pl.BlockSpec((tm, tn), lambda j, i: (i, j))
    out_shape = jax.ShapeDtypeStruct((t, n), out_dtype)
    if rope is None:
        return pl.pallas_call(_proj_kernel, grid=grid, in_specs=[h_spec, w_spec], out_specs=o_spec,
                              out_shape=out_shape, compiler_params=_params("parallel", "parallel"),
                              name="proj")(h, w)
    gain, cos_t, sin_t, seg2 = rope
    nseq = seq // tm
    return pl.pallas_call(
        _proj_rope_kernel, grid=grid,
        in_specs=[h_spec, w_spec,
                  pl.BlockSpec((1, tn), lambda j, i: (0, j)),
                  pl.BlockSpec((tm, LANES), lambda j, i: (i % nseq, 0)),
                  pl.BlockSpec((tm, LANES), lambda j, i: (i % nseq, 0)),
                  pl.BlockSpec((2 * LANES, LANES), lambda j, i: (0, 0))],
        out_specs=o_spec, out_shape=out_shape,
        compiler_params=_params("parallel", "parallel"), name="proj_rope",
    )(h, w, gain, cos_t, sin_t, seg2)


def _band_kernel(q_ref, kp_ref, kc_ref, vp_ref, vc_ref, o_ref, lse_ref):
    i = pl.program_id(2)
    row = lax.broadcasted_iota(I32, (TQ, 2 * TQ), 0)
    col = lax.broadcasted_iota(I32, (TQ, 2 * TQ), 1)
    dist = TQ + row - col
    valid = (dist >= 0) & (dist <= A_BAND) & ((col >= TQ) | (i > 0))
    lane = lax.broadcasted_iota(I32, (TQ, LANES), 1)
    low = lane < HEAD_DIM
    lse_tile = jnp.zeros((TQ, LANES), F32)
    for p in range(A_HEADS_PER_GROUP // 2):
        sl = slice(p * LANES, (p + 1) * LANES)
        q2 = q_ref[:, sl].astype(F32)
        k2 = jnp.concatenate([kp_ref[:, sl], kc_ref[:, sl]], axis=0)
        v2 = jnp.concatenate([vp_ref[:, sl], vc_ref[:, sl]], axis=0)
        outs = []
        for a in range(2):
            qa = jnp.where(low if a == 0 else jnp.logical_not(low), q2, 0.0).astype(BF16)
            s = _dot_nt(qa, k2) * (HEAD_DIM ** -0.5)
            s = jnp.where(valid, s, NEG)
            m = jnp.max(s, axis=-1, keepdims=True)
            e = jnp.exp(s - m)
            l = jnp.sum(e, axis=-1, keepdims=True)
            outs.append(_dot(e.astype(BF16), v2) / l)
            lse_tile = jnp.where(lane == 2 * p + a, m + jnp.log(l), lse_tile)
        o_ref[:, sl] = jnp.where(low, outs[0], outs[1])
    lse_ref[...] = lse_tile


def _band_attention(aqk, av, group, dil, batch, seq):
    t = aqk.shape[0]
    sub = seq // dil
    nb = sub // TQ
    rows = t // dil
    qk = aqk.reshape(rows, dil * aqk.shape[1])
    v = av.reshape(rows, dil * av.shape[1])
    nqk, nv = aqk.shape[1] // A_OUT, av.shape[1] // A_OUT

    def cur(b, r, i):
        return b * nb + i

    def prev(b, r, i):
        return b * nb + jnp.maximum(i - 1, 0)

    o, lse = pl.pallas_call(
        _band_kernel,
        grid=(batch, dil, nb),
        in_specs=[
            pl.BlockSpec((TQ, A_OUT), lambda b, r, i: (cur(b, r, i), r * nqk + group)),
            pl.BlockSpec((TQ, A_OUT), lambda b, r, i: (prev(b, r, i), r * nqk + 3 + group)),
            pl.BlockSpec((TQ, A_OUT), lambda b, r, i: (cur(b, r, i), r * nqk + 3 + group)),
            pl.BlockSpec((TQ, A_OUT), lambda b, r, i: (prev(b, r, i), r * nv + group)),
            pl.BlockSpec((TQ, A_OUT), lambda b, r, i: (cur(b, r, i), r * nv + group)),
        ],
        out_specs=[pl.BlockSpec((TQ, A_OUT), lambda b, r, i: (cur(b, r, i), r)),
                   pl.BlockSpec((TQ, LANES), lambda b, r, i: (cur(b, r, i), r))],
        out_shape=[jax.ShapeDtypeStruct((rows, dil * A_OUT), F32),
                   jax.ShapeDtypeStruct((rows, dil * LANES), F32)],
        compiler_params=_params("parallel", "parallel", "parallel"),
        name=f"band_attention_d{dil}",
    )(qk, qk, qk, v, v)
    return o.reshape(t, A_OUT), lse.reshape(t, LANES)


def _compress_kernel(x_ref, pe_ref, w1_ref, w2_ref, gain_ref, cos_ref, sin_ref, seg_ref, o_ref):
    kv = pl.program_id(0)
    x = x_ref[...]
    n = x.shape[0]
    top = _dot((x + pe_ref[0:1, :]).astype(BF16), w1_ref[0])
    bot = _dot((x + pe_ref[1:2, :]).astype(BF16), w1_ref[1])
    hid = jax.nn.gelu(top + pltpu.roll(bot, n - 1, 0))
    out = _dot(hid.astype(BF16), w2_ref[...])

    @pl.when(kv == 0)
    def _():
        o_ref[...] = _head_norm_rope(out, gain_ref[...], cos_ref[...], sin_ref[...], seg_ref[...]).astype(o_ref.dtype)

    @pl.when(kv != 0)
    def _():
        o_ref[...] = out.astype(o_ref.dtype)


def _compress(raw, pe, w1e, w2e, gain, cos_c, sin_c, seg2, batch):
    n_chunk = raw.shape[1] // batch
    width = raw.shape[2]
    return pl.pallas_call(
        _compress_kernel,
        grid=(2, batch),
        in_specs=[
            pl.BlockSpec((None, n_chunk, width), lambda kv, b: (kv, b, 0)),
            pl.BlockSpec((None, 2, width), lambda kv, b: (kv, 0, 0)),
            pl.BlockSpec((None, 2, width, 2 * CMP_HIDDEN), lambda kv, b: (kv, 0, 0, 0)),
            pl.BlockSpec((None, 2 * CMP_HIDDEN, LANES), lambda kv, b: (kv, 0, 0)),
            pl.BlockSpec((1, LANES), lambda kv, b: (0, 0)),
            pl.BlockSpec((n_chunk, LANES), lambda kv, b: (0, 0)),
            pl.BlockSpec((n_chunk, LANES), lambda kv, b: (0, 0)),
            pl.BlockSpec((2 * LANES, LANES), lambda kv, b: (0, 0)),
        ],
        out_specs=pl.BlockSpec((None, n_chunk, LANES), lambda kv, b: (kv, b, 0)),
        out_shape=jax.ShapeDtypeStruct((2, raw.shape[1], LANES), BF16),
        compiler_params=_params("parallel", "parallel"),
        name="compress",
    )(raw, pe, w1e, w2e, gain, cos_c, sin_c, seg2)


def _softmax_rows(s, valid):
    s = jnp.where(valid, s, NEG)
    m = jnp.max(s, axis=-1, keepdims=True)
    e = jnp.exp(s - m)
    return e, jnp.sum(e, axis=-1, keepdims=True)


def _nsa_kernel(q_ref, ksel_ref, vsel_ref, kwin_ref, vwin_ref, kc_ref, vc_ref, gate_ref, wslc_ref, o_ref,
                *, n_cmp):
    qi = pl.program_id(1)
    q0 = qi * TQ
    rows = B_GQA * TQ
    lane = lax.broadcasted_iota(I32, (TQ, LANES), 1)
    low = lane < HEAD_DIM
    t1 = q0 + lax.broadcasted_iota(I32, (TQ, 1), 0)
    t4 = jnp.concatenate([t1] * B_GQA, axis=0)
    gates = jax.nn.sigmoid(gate_ref[...])
    n_ck = kc_ref.shape[0]

    for h in range(B_KV_HEADS):
        pieces = []
        for g in range(B_GQA):
            c, half = (h * B_GQA + g) // 2, g % 2
            chunk = q_ref[:, c * LANES:(c + 1) * LANES].astype(F32)
            if half != h:
                chunk = pltpu.roll(chunk, HEAD_DIM, 1)
            chunk = jnp.where(low if h == 0 else jnp.logical_not(low), chunk, 0.0)
            pieces.append((chunk * (HEAD_DIM ** -0.5)).astype(BF16))
        qp = jnp.concatenate(pieces, axis=0)

        def gate_col(branch):
            cols = [gates[:, branch * B_Q_HEADS + h * B_GQA + g:branch * B_Q_HEADS + h * B_GQA + g + 1]
                    for g in range(B_GQA)]
            return jnp.concatenate(cols, axis=0)

        s = _dot_nt(qp, kc_ref[...])
        cidx = lax.broadcasted_iota(I32, (rows, n_ck), 1)
        valid = (cidx * CMP_STRIDE + (CMP_LEN - 1) <= t4) & (cidx < n_cmp)
        s = jnp.where(valid, s, NEG)
        m = jnp.max(s, axis=-1, keepdims=True)
        e = jnp.where(valid, jnp.exp(s - m), 0.0)
        p = e / jnp.maximum(jnp.sum(e, axis=-1, keepdims=True), TINY)
        o_tot = gate_col(0) * _dot(p.astype(BF16), vc_ref[...])

        imp = p[0:TQ]
        for g in range(1, B_GQA):
            imp = imp + p[g * TQ:(g + 1) * TQ]
        p_slc = _split_dot(imp, wslc_ref[...])
        cur = jnp.right_shift(t1, SEL_SHIFT)
        forced = (lane == 0) | (lane == cur) | (lane == cur - 1)
        score = jnp.where(forced, FORCE_SCORE, jnp.where(lane * SEL_LEN <= t1, p_slc, -1.0))
        lane_f = lane.astype(F32)
        chosen = jnp.zeros((TQ, LANES), F32)
        for _ in range(N_SELECT):
            best = jnp.max(score, axis=-1, keepdims=True)
            first = jnp.min(jnp.where(score == best, lane_f, float(LANES)), axis=-1, keepdims=True)
            pick = lane_f == first
            chosen = jnp.where(pick, 1.0, chosen)
            score = jnp.where(pick, -2.0, score)
        bias = jnp.where(chosen > 0.0, 0.0, NEG).astype(BF16)
        bias4 = jnp.concatenate([bias] * B_GQA, axis=0)

        blk_row = lax.broadcasted_iota(I32, (LANES, SEL_TK), 0)
        blk_col = jnp.right_shift(lax.broadcasted_iota(I32, (LANES, SEL_TK), 1), SEL_SHIFT)
        key_col = lax.broadcasted_iota(I32, (rows, SEL_TK), 1)

        def sel_tile(kt, carry, causal):
            m_i, l_i, acc = carry
            k0 = pl.multiple_of(kt * SEL_TK, SEL_TK)
            expand = jnp.where(blk_col + kt * (SEL_TK // SEL_LEN) == blk_row, 1.0, 0.0).astype(BF16)
            s = _dot_nt(qp, ksel_ref[pl.ds(k0, SEL_TK), :]) + _dot(bias4, expand)
            if causal:
                s = jnp.where(k0 + key_col <= t4, s, NEG)
            m_new = jnp.maximum(m_i, jnp.max(s, axis=-1, keepdims=True))
            alpha = jnp.exp(m_i - m_new)
            pe = jnp.exp(s - m_new)
            l_new = alpha * l_i + jnp.sum(pe, axis=-1, keepdims=True)
            acc = alpha * acc + _dot(pe.astype(BF16), vsel_ref[pl.ds(k0, SEL_TK), :])
            return m_new, l_new, acc

        init = (jnp.full((rows, 1), NEG, F32), jnp.zeros((rows, 1), F32), jnp.zeros((rows, LANES), F32))
        n_full = q0 // SEL_TK
        carry = lax.fori_loop(0, n_full, lambda kt, c: sel_tile(kt, c, False), init)
        _, l_s, acc_s = sel_tile(n_full, carry, True)
        o_tot = o_tot + gate_col(1) * (acc_s / l_s)

        span = WIN + TQ
        w0 = pl.multiple_of(jnp.maximum(q0 - WIN, 0), TQ)
        s = _dot_nt(qp, kwin_ref[pl.ds(w0, span), :])
        dist = t4 - (w0 + lax.broadcasted_iota(I32, (rows, span), 1))
        e, l_w = _softmax_rows(s, (dist >= 0) & (dist < WIN))
        o_tot = o_tot + gate_col(2) * (_dot(e.astype(BF16), vwin_ref[pl.ds(w0, span), :]) / l_w)

        for c2 in range(B_GQA // 2):
            halves = []
            for half in range(2):
                g = 2 * c2 + half
                piece = o_tot[g * TQ:(g + 1) * TQ]
                halves.append(piece if half == h else pltpu.roll(piece, HEAD_DIM, 1))
            c = h * (B_GQA // 2) + c2
            o_ref[:, c * LANES:(c + 1) * LANES] = jnp.where(low, halves[0], halves[1]).astype(o_ref.dtype)


def _nsa(bqk, bv, cmp_kv, fgate, wslc2, batch, seq, gate_block):
    t = bqk.shape[0]
    nq = seq // TQ
    n_ck = cmp_kv.shape[1] // batch
    n_cmp = seq // CMP_STRIDE - CMP_LEN // CMP_STRIDE + 1
    whole = lambda col: pl.BlockSpec((seq, LANES), lambda b, i: (b, col))
    return pl.pallas_call(
        functools.partial(_nsa_kernel, n_cmp=n_cmp),
        grid=(batch, nq),
        in_specs=[
            pl.BlockSpec((TQ, B_QD), lambda b, i: (b * nq + i, 0)),
            whole(B_QD // LANES), pl.BlockSpec((seq, LANES), lambda b, i: (b, 0)),
            whole(B_QD // LANES + 1), pl.BlockSpec((seq, LANES), lambda b, i: (b, 1)),
            pl.BlockSpec((None, n_ck, LANES), lambda b, i: (0, b, 0)),
            pl.BlockSpec((None, n_ck, LANES), lambda b, i: (1, b, 0)),
            pl.BlockSpec((TQ, LANES), lambda b, i: (b * nq + i, gate_block)),
            pl.BlockSpec(wslc2.shape, lambda b, i: (0, 0)),
        ],
        out_specs=pl.BlockSpec((TQ, B_OUT), lambda b, i: (b * nq + i, 0)),
        out_shape=jax.ShapeDtypeStruct((t, B_OUT), BF16),
        compiler_params=_params("parallel", "parallel"),
        name="nsa_mixer",
    )(bqk, bqk, bv, bqk, bv, cmp_kv, cmp_kv, fgate, wslc2)


def _merge_kernel(o0_ref, o1_ref, o2_ref, l0_ref, l1_ref, l2_ref, ob_ref, m0_ref, m1_ref, x_ref,
                  wpa_ref, wpb_ref, wout_ref, exp_ref, g_ref, xo_ref, h_ref):
    l0, l1, l2 = l0_ref[...], l1_ref[...], l2_ref[...]
    mx = jnp.maximum(jnp.maximum(l0, l1), l2)
    e0, e1, e2 = jnp.exp(l0 - mx), jnp.exp(l1 - mx), jnp.exp(l2 - mx)
    den = e0 + e1 + e2
    ex = exp_ref[...]
    out_a = (_split_dot(e0 / den, ex) * o0_ref[...] + _split_dot(e1 / den, ex) * o1_ref[...]
             + _split_dot(e2 / den, ex) * o2_ref[...])
    pa = _dot(out_a.astype(BF16), wpa_ref[...])
    pb = _dot(ob_ref[...], wpb_ref[...])
    mixed = jax.nn.sigmoid(m0_ref[...]) * pa + jax.nn.sigmoid(m1_ref[...]) * pb
    x = x_ref[...] + _dot(mixed.astype(BF16), wout_ref[...])
    xo_ref[...] = x
    y = x * lax.rsqrt(jnp.mean(x * x, axis=-1, keepdims=True) + EPS)
    h_ref[...] = (y * g_ref[...]).astype(h_ref.dtype)


def _merge(o_groups, lse_groups, out_b, fgate, x, wpa, wpb, wout, expand2, g_next, tm=256):
    t, d = x.shape
    row = lambda w: pl.BlockSpec((tm, w), lambda i: (i, 0))
    full = lambda a: pl.BlockSpec(a.shape, lambda i: (0,) * a.ndim)
    g_next = g_next.reshape(1, d)
    return pl.pallas_call(
        _merge_kernel,
        grid=(t // tm,),
        in_specs=[row(A_OUT)] * 3 + [row(LANES)] * 3 + [row(B_OUT),
                  pl.BlockSpec((tm, d), lambda i: (i, 0)), pl.BlockSpec((tm, d), lambda i: (i, 1)), row(d),
                  full(wpa), full(wpb), full(wout), full(expand2), full(g_next)],
        out_specs=[row(d), row(d)],
        out_shape=[jax.ShapeDtypeStruct((t, d), F32), jax.ShapeDtypeStruct((t, d), BF16)],
        compiler_params=_params("parallel"),
        name="merge_out",
    )(*o_groups, *lse_groups, out_b, fgate, fgate, x, wpa, wpb, wout, expand2, g_next)


def _ffn_up_kernel(h_ref, wg_ref, wu_ref, cw_ref, cb_ref, o_ref, gbuf_ref, *, tiles_per_seq):
    i = pl.program_id(1)
    tm = h_ref.shape[0]
    h = h_ref[...]

    @pl.when(i % tiles_per_seq == 0)
    def _():
        gbuf_ref[0:8, :] = jnp.zeros((8, gbuf_ref.shape[1]), F32)

    @pl.when(i % tiles_per_seq != 0)
    def _():
        gbuf_ref[0:8, :] = gbuf_ref[tm:tm + 8, :]

    gbuf_ref[8:tm + 8, :] = _dot(h, wg_ref[...])
    u = _dot(h, wu_ref[...])
    conv = cb_ref[...] + cw_ref[0:1, :] * gbuf_ref[pl.ds(6, tm), :]
    conv = conv + cw_ref[1:2, :] * gbuf_ref[pl.ds(7, tm), :]
    conv = conv + cw_ref[2:3, :] * gbuf_ref[pl.ds(8, tm), :]
    o_ref[...] = (jax.nn.silu(conv) * u).astype(o_ref.dtype)


def _ffn_up(h, w_up, conv_w, conv_b, seq, tm=512, tn=1408):
    t, d = h.shape
    nj = D_FF // tn
    return pl.pallas_call(
        functools.partial(_ffn_up_kernel, tiles_per_seq=seq // tm),
        grid=(nj, t // tm),
        in_specs=[
            pl.BlockSpec((tm, d), lambda j, i: (i, 0)),
            pl.BlockSpec((d, tn), lambda j, i: (0, j)),
            pl.BlockSpec((d, tn), lambda j, i: (0, nj + j)),
            pl.BlockSpec((CONV_W, tn), lambda j, i: (0, j)),
            pl.BlockSpec((1, tn), lambda j, i: (0, j)),
        ],
        out_specs=pl.BlockSpec((tm, tn), lambda j, i: (i, j)),
        out_shape=jax.ShapeDtypeStruct((t, D_FF), BF16),
        scratch_shapes=[pltpu.VMEM((tm + 8, tn), F32)],
        compiler_params=_params("arbitrary", "arbitrary"),
        name="ffn_up_conv",
    )(h, w_up, w_up, conv_w, conv_b.reshape(1, D_FF))


def _ffn_down_kernel(a_ref, w_ref, x_ref, g_ref, xo_ref, h_ref):
    x = x_ref[...] + _dot(a_ref[...], w_ref[...])
    xo_ref[...] = x
    y = x * lax.rsqrt(jnp.mean(x * x, axis=-1, keepdims=True) + EPS)
    h_ref[...] = (y * g_ref[...]).astype(h_ref.dtype)


def _ffn_down(act, w_down, x, g_next, tm=512):
    t, d = x.shape
    row = lambda w: pl.BlockSpec((tm, w), lambda i: (i, 0))
    return pl.pallas_call(
        _ffn_down_kernel,
        grid=(t // tm,),
        in_specs=[row(D_FF), pl.BlockSpec(w_down.shape, lambda i: (0, 0)), row(d),
                  pl.BlockSpec((1, d), lambda i: (0, 0))],
        out_specs=[row(d), row(d)],
        out_shape=[jax.ShapeDtypeStruct((t, d), F32), jax.ShapeDtypeStruct((t, d), BF16)],
        compiler_params=_params("parallel"),
        name="ffn_down",
    )(act, w_down, x, g_next.reshape(1, d))


def _rope_tables(pos):
    half = HEAD_DIM // 2
    inv_freq = ROPE_THETA ** (-jnp.arange(half, dtype=F32) / half)
    ang = pos.astype(F32)[:, None] * inv_freq[None, :]
    c, s = jnp.cos(ang), jnp.sin(ang)
    return jnp.tile(c, (1, 4)), jnp.tile(jnp.concatenate([-s, s], axis=1), (1, 2))


def _constants(n_ck):
    lane = np.arange(LANES)
    seg = (lane[:, None] // HEAD_DIM == lane[None, :] // HEAD_DIM).astype(np.float32)
    seg2 = np.concatenate([seg, seg], axis=0)
    ratio = SEL_LEN // CMP_STRIDE
    wslc = np.zeros((n_ck, LANES), np.float32)
    for off in range(CMP_LEN // CMP_STRIDE):
        for i in range(n_ck):
            if i + off < n_ck:
                wslc[i, (i + off) // ratio] += 1.0
    wslc2 = np.concatenate([wslc, wslc], axis=0)
    expand = np.zeros((LANES, A_OUT), np.float32)
    for j in range(A_HEADS_PER_GROUP):
        expand[j, j * HEAD_DIM:(j + 1) * HEAD_DIM] = 1.0
    expand2 = np.concatenate([expand, expand], axis=0)
    return (jnp.asarray(seg2, BF16), jnp.asarray(wslc2, BF16), jnp.asarray(expand2, BF16))


def kernel(x, norm_mix_g, w_in, a_q_g, a_k_g, b_q_g, b_k_g, cmp_pos, cmp_w1, cmp_w2, w_proj_a, w_proj_b,
           w_out, norm_ffn_g, w_up, conv_w, conv_b, w_down):
    batch, seq, d = x.shape
    depth = w_in.shape[0]
    t = batch * seq
    assert d == D_MODEL and seq % (16 * TQ) == 0 and seq // SEL_LEN <= LANES
    n_ck = seq // CMP_STRIDE
    n_cmp = n_ck - CMP_LEN // CMP_STRIDE + 1
    seg2, wslc2, expand2 = _constants(n_ck)
    cos_t, sin_t = _rope_tables(jnp.arange(seq))
    cos_c, sin_c = _rope_tables(jnp.arange(n_ck) * CMP_STRIDE + CMP_LEN - 1)
    eye_h = jnp.eye(B_KV_HEADS, dtype=F32)
    o_kq, o_kv, o_gate, o_merge = A_QKV, A_QKV + B_QD, A_QKV + B_QD + B_KV, A_QKV + B_QD + B_KV + B_GATE
    f_gate_block = 2 * D_MODEL // LANES

    xf = x.reshape(t, d)
    h = _rmsnorm(xf, norm_mix_g[0])
    for l in range(depth):
        w = w_in[l]
        kv = lambda i: w[:, o_kv + i * LANES:o_kv + (i + 1) * LANES]
        w_aqk = w[:, :2 * A_HEADS * HEAD_DIM].astype(BF16)
        w_av = w[:, 2 * A_HEADS * HEAD_DIM:A_QKV].astype(BF16)
        w_bqk = jnp.concatenate([w[:, o_kq:o_kv], kv(2), kv(4)], axis=1).astype(BF16)
        w_bv = jnp.concatenate([kv(3), kv(5)], axis=1).astype(BF16)
        w_f = jnp.concatenate([w[:, o_merge:], w[:, o_gate:o_merge],
                               jnp.zeros((d, LANES - B_GATE), F32), kv(0), kv(1)], axis=1).astype(BF16)
        gain_a = jnp.concatenate([jnp.tile(a_q_g[l], A_HEADS), jnp.tile(a_k_g[l], A_HEADS)]).reshape(1, -1)
        gain_b = jnp.concatenate([jnp.tile(b_q_g[l], B_Q_HEADS), jnp.tile(b_k_g[l, 1], B_KV_HEADS),
                                  jnp.tile(b_k_g[l, 2], B_KV_HEADS)]).reshape(1, -1)

        aqk = _proj(h, w_aqk, BF16, tn=768, rope=(gain_a, cos_t, sin_t, seg2), seq=seq)
        av = _proj(h, w_av, BF16, tn=768)
        bqk = _proj(h, w_bqk, BF16, tn=768, rope=(gain_b, cos_t, sin_t, seg2), seq=seq)
        bv = _proj(h, w_bv, BF16, tn=256)
        fg = _proj(h, w_f, F32, tn=w_f.shape[1], tm=256)

        o_groups, lse_groups = [], []
        for gi, (_, dil) in enumerate(DIL_PAIRS):
            o_g, lse_g = _band_attention(aqk, av, gi, dil, batch, seq)
            o_groups.append(o_g)
            lse_groups.append(lse_g)

        raw = fg[:, 2 * D_MODEL + LANES:].reshape(t // CMP_STRIDE, CMP_STRIDE, 2, LANES)
        raw = raw.transpose(2, 0, 1, 3).reshape(2, t // CMP_STRIDE, CMP_STRIDE * LANES)
        pe = jnp.tile(cmp_pos[l].reshape(2, 2, CMP_STRIDE, 1, HEAD_DIM), (1, 1, 1, B_KV_HEADS, 1))
        pe = pe.reshape(2, 2, CMP_STRIDE * LANES)
        w1 = cmp_w1[l].reshape(2, 2, CMP_STRIDE, HEAD_DIM, CMP_HIDDEN)
        w1e = jnp.einsum('kspdn,hj->kspjdhn', w1, eye_h).reshape(2, 2, CMP_STRIDE * LANES, 2 * CMP_HIDDEN)
        w2e = jnp.einsum('knd,hj->khnjd', cmp_w2[l], eye_h).reshape(2, 2 * CMP_HIDDEN, LANES)
        gain_c = jnp.tile(b_k_g[l, 0], B_KV_HEADS).reshape(1, LANES)
        cmp_kv = _compress(raw, pe, w1e.astype(BF16), w2e.astype(BF16), gain_c, cos_c, sin_c, seg2, batch)
        out_b = _nsa(bqk, bv, cmp_kv, fg, wslc2, batch, seq, f_gate_block)

        xf, h = _merge(o_groups, lse_groups, out_b, fg, xf, w_proj_a[l].astype(BF16), w_proj_b[l].astype(BF16),
                       w_out[l].astype(BF16), expand2, norm_ffn_g[l])
        act = _ffn_up(h, w_up[l].astype(BF16), conv_w[l], conv_b[l], seq)
        g_next = norm_mix_g[l + 1] if l + 1 < depth else jnp.ones((d,), F32)
        xf, h = _ffn_down(act, w_down[l].astype(BF16), xf, g_next)
    return xf.reshape(batch, seq, d)
```

```python
import functools

import numpy as np
import jax
import jax.numpy as jnp
from jax import lax
from jax.experimental import pallas as pl
from jax.experimental.pallas import tpu as pltpu

F32 = jnp.float32
BF16 = jnp.bfloat16
I32 = jnp.int32

D_MODEL = 1024
HEAD_DIM = 64
ROPE_THETA = 10000.0
EPS = 1e-6
NEG = -1e30
TINY = 1e-30
FORCE_SCORE = 1e9

DIL_PAIRS = ((128, 1), (512, 4), (2048, 16))
A_HEADS_PER_GROUP = 8
A_HEADS = 24
A_OUT = 512
A_BAND = 128

B_Q_HEADS = 8
B_KV_HEADS = 2
B_GQA = 4
CMP_LEN = 32
CMP_STRIDE = 16
CMP_HIDDEN = 256
SEL_LEN = 64
SEL_SHIFT = 6
N_SELECT = 16
WIN = 512
B_OUT = 512

D_FF = 2816
CONV_W = 3

A_QKV = 3 * A_HEADS * HEAD_DIM
B_QD = B_Q_HEADS * HEAD_DIM
B_KV = 3 * 2 * B_KV_HEADS * HEAD_DIM
B_GATE = 3 * B_Q_HEADS

LANES = 128
TQ = 128
SEL_TK = 512
VMEM_LIMIT = 48 * 1024 * 1024

_NT = (((1,), (1,)), ((), ()))


def _dot(a, b):
    return jnp.dot(a, b, preferred_element_type=F32)


def _dot_nt(a, b):
    return lax.dot_general(a, b, _NT, preferred_element_type=F32)


def _split(x):
    hi = x.astype(BF16)
    lo = (x - hi.astype(F32)).astype(BF16)
    return jnp.concatenate([hi, lo], axis=1)


def _split_dot(x, w2):
    return _dot(_split(x), w2)


def _params(*sem):
    return pltpu.CompilerParams(dimension_semantics=sem, vmem_limit_bytes=VMEM_LIMIT)


def _rmsnorm_kernel(x_ref, g_ref, h_ref):
    x = x_ref[...]
    y = x * lax.rsqrt(jnp.mean(x * x, axis=-1, keepdims=True) + EPS)
    h_ref[...] = (y * g_ref[...]).astype(h_ref.dtype)


def _rmsnorm(x, g, tm=512):
    t, d = x.shape
    return pl.pallas_call(
        _rmsnorm_kernel,
        grid=(t // tm,),
        in_specs=[pl.BlockSpec((tm, d), lambda i: (i, 0)), pl.BlockSpec((1, d), lambda i: (0, 0))],
        out_specs=pl.BlockSpec((tm, d), lambda i: (i, 0)),
        out_shape=jax.ShapeDtypeStruct((t, d), BF16),
        compiler_params=_params("parallel"),
        name="rmsnorm",
    )(x, g.reshape(1, d))


def _head_norm_rope(y, gain, cos_t, sin_t, seg2):
    ss = _split_dot(y * y, seg2)
    yn = y * lax.rsqrt(ss * (1.0 / HEAD_DIM) + EPS) * gain
    lane = lax.broadcasted_iota(I32, yn.shape, 1)
    partner = jnp.where((lane & 32) == 0, pltpu.roll(yn, 96, 1), pltpu.roll(yn, 32, 1))
    return yn * cos_t + partner * sin_t


def _proj_kernel(h_ref, w_ref, o_ref):
    o_ref[...] = _dot(h_ref[...], w_ref[...]).astype(o_ref.dtype)


def _proj_rope_kernel(h_ref, w_ref, gain_ref, cos_ref, sin_ref, seg_ref, o_ref):
    y = _dot(h_ref[...], w_ref[...])
    cos_t, sin_t, seg2 = cos_ref[...], sin_ref[...], seg_ref[...]
    for c in range(y.shape[1] // LANES):
        sl = slice(c * LANES, (c + 1) * LANES)
        o_ref[:, sl] = _head_norm_rope(y[:, sl], gain_ref[:, sl], cos_t, sin_t, seg2).astype(o_ref.dtype)


def _proj(h, w, out_dtype, tn, tm=512, rope=None, seq=None):
    t, k = h.shape
    n = w.shape[1]
    grid = (n // tn, t // tm)
    h_spec = pl.BlockSpec((tm, k), lambda j, i: (i, 0))
    w_spec = pl.BlockSpec((k, tn), lambda j, i: (0, j))
    o_spec = pl.BlockSpec((tm, tn), lambda j, i: (i, j))
    out_shape = jax.ShapeDtypeStruct((t, n), out_dtype)
    if rope is None:
        return pl.pallas_call(_proj_kernel, grid=grid, in_specs=[h_spec, w_spec], out_specs=o_spec,
                              out_shape=out_shape, compiler_params=_params("parallel", "parallel"),
                              name="proj")(h, w)
    gain, cos_t, sin_t, seg2 = rope
    nseq = seq // tm
    return pl.pallas_call(
        _proj_rope_kernel, grid=grid,
        in_specs=[h_spec, w_spec,
                  pl.BlockSpec((1, tn), lambda j, i: (0, j)),
                  pl.BlockSpec((tm, LANES), lambda j, i: (i % nseq, 0)),
                  pl.BlockSpec((tm, LANES), lambda j, i: (i % nseq, 0)),
                  pl.BlockSpec((2 * LANES, LANES), lambda j, i: (0, 0))],
        out_specs=o_spec, out_shape=out_shape,
        compiler_params=_params("parallel", "parallel"), name="proj_rope",
    )(h, w, gain, cos_t, sin_t, seg2)


def _proj_group_kernel(h_ref, w_ref, gain_ref, cos_ref, sin_ref, seg_ref, o_ref, y_ref, *, n_rope):
    dil, n_sub, _ = o_ref.shape
    y = _dot(h_ref[...], w_ref[...])
    cos_t, sin_t, seg2 = cos_ref[...], sin_ref[...], seg_ref[...]
    for c in range(y.shape[1] // LANES):
        sl = slice(c * LANES, (c + 1) * LANES)
        val = _head_norm_rope(y[:, sl], gain_ref[:, sl], cos_t, sin_t, seg2) if c < n_rope else y[:, sl]
        if dil == 1:
            o_ref[0, :, sl] = val.astype(o_ref.dtype)
        else:
            y_ref[c] = val
            for r in range(dil):
                o_ref[r, :, sl] = y_ref[c, pl.ds(r, n_sub, stride=dil), :].astype(o_ref.dtype)


def _proj_group(h, w, gain, cos_t, sin_t, seg2, dil, batch, seq, tm=512):
    t, k = h.shape
    n = w.shape[1]
    nseq = seq // tm
    return pl.pallas_call(
        functools.partial(_proj_group_kernel, n_rope=gain.shape[1] // LANES),
        grid=(t // tm,),
        in_specs=[pl.BlockSpec((tm, k), lambda i: (i, 0)),
                  pl.BlockSpec((k, n), lambda i: (0, 0)),
                  pl.BlockSpec(gain.shape, lambda i: (0, 0)),
                  pl.BlockSpec((tm, LANES), lambda i: (i % nseq, 0)),
                  pl.BlockSpec((tm, LANES), lambda i: (i % nseq, 0)),
                  pl.BlockSpec((2 * LANES, LANES), lambda i: (0, 0))],
        out_specs=pl.BlockSpec((None, dil, tm // dil, n), lambda i: (i // nseq, 0, i % nseq, 0)),
        out_shape=jax.ShapeDtypeStruct((batch, dil, seq // dil, n), BF16),
        scratch_shapes=[pltpu.VMEM((n // LANES, tm, LANES), F32)],
        compiler_params=_params("parallel"),
        name=f"proj_group_d{dil}",
    )(h, w, gain, cos_t, sin_t, seg2)


def _band_kernel(q_ref, kp_ref, kc_ref, vp_ref, vc_ref, o_ref, lse_ref):
    i = pl.program_id(1)
    row = lax.broadcasted_iota(I32, (TQ, 2 * TQ), 0)
    col = lax.broadcasted_iota(I32, (TQ, 2 * TQ), 1)
    dist = TQ + row - col
    valid = (dist >= 0) & (dist <= A_BAND) & ((col >= TQ) | (i > 0))
    lane = lax.broadcasted_iota(I32, (TQ, LANES), 1)
    low = lane < HEAD_DIM
    lse_tile = jnp.zeros((TQ, LANES), F32)
    for p in range(A_HEADS_PER_GROUP // 2):
        sl = slice(p * LANES, (p + 1) * LANES)
        q2 = q_ref[:, sl].astype(F32)
        k2 = jnp.concatenate([kp_ref[:, sl], kc_ref[:, sl]], axis=0)
        v2 = jnp.concatenate([vp_ref[:, sl], vc_ref[:, sl]], axis=0)
        outs = []
        for a in range(2):
            qa = jnp.where(low if a == 0 else jnp.logical_not(low), q2, 0.0).astype(BF16)
            s = _dot_nt(qa, k2) * (HEAD_DIM ** -0.5)
            s = jnp.where(valid, s, NEG)
            m = jnp.max(s, axis=-1, keepdims=True)
            e = jnp.exp(s - m)
            l = jnp.sum(e, axis=-1, keepdims=True)
            outs.append(_dot(e.astype(BF16), v2) / l)
            lse_tile = jnp.where(lane == 2 * p + a, m + jnp.log(l), lse_tile)
        o_ref[:, sl] = jnp.where(low, outs[0], outs[1])
    lse_ref[...] = lse_tile


def _band_attention(qkv, dil, batch, seq):
    sub = seq // dil
    nb = sub // TQ
    n_sub = batch * dil
    flat = qkv.reshape(n_sub * sub, qkv.shape[-1])

    def cur(n, i):
        return n * nb + i

    def prev(n, i):
        return n * nb + jnp.maximum(i - 1, 0)

    o, lse = pl.pallas_call(
        _band_kernel,
        grid=(n_sub, nb),
        in_specs=[
            pl.BlockSpec((TQ, A_OUT), lambda n, i: (cur(n, i), 0)),
            pl.BlockSpec((TQ, A_OUT), lambda n, i: (prev(n, i), 1)),
            pl.BlockSpec((TQ, A_OUT), lambda n, i: (cur(n, i), 1)),
            pl.BlockSpec((TQ, A_OUT), lambda n, i: (prev(n, i), 2)),
            pl.BlockSpec((TQ, A_OUT), lambda n, i: (cur(n, i), 2)),
        ],
        out_specs=[pl.BlockSpec((TQ, A_OUT), lambda n, i: (cur(n, i), 0)),
                   pl.BlockSpec((TQ, LANES), lambda n, i: (cur(n, i), 0))],
        out_shape=[jax.ShapeDtypeStruct((n_sub * sub, A_OUT), F32),
                   jax.ShapeDtypeStruct((n_sub * sub, LANES), F32)],
        compiler_params=_params("parallel", "parallel"),
        name=f"band_attention_d{dil}",
    )(flat, flat, flat, flat, flat)
    return o.reshape(batch, dil, sub, A_OUT), lse.reshape(batch, dil, sub, LANES)


def _compress_kernel(x_ref, pe_ref, w1_ref, w2_ref, gain_ref, cos_ref, sin_ref, seg_ref, o_ref):
    kv = pl.program_id(0)
    n = o_ref.shape[0]
    top = jnp.zeros((n, w1_ref.shape[-1]), F32)
    bot = jnp.zeros((n, w1_ref.shape[-1]), F32)
    for p in range(CMP_STRIDE):
        xp = x_ref[pl.ds(p, n, stride=CMP_STRIDE), :]
        top = top + _dot((xp + pe_ref[p:p + 1, :]).astype(BF16), w1_ref[p])
        bot = bot + _dot((xp + pe_ref[CMP_STRIDE + p:CMP_STRIDE + p + 1, :]).astype(BF16), w1_ref[CMP_STRIDE + p])
    hid = jax.nn.gelu(top + pltpu.roll(bot, n - 1, 0))
    out = _dot(hid.astype(BF16), w2_ref[...])

    @pl.when(kv == 0)
    def _():
        o_ref[...] = _head_norm_rope(out, gain_ref[...], cos_ref[...], sin_ref[...], seg_ref[...]).astype(o_ref.dtype)

    @pl.when(kv != 0)
    def _():
        o_ref[...] = out.astype(o_ref.dtype)


def _compress(fg, raw_block, pe, w1e, w2e, gain, cos_c, sin_c, seg2, batch, seq):
    n_chunk = seq // CMP_STRIDE
    return pl.pallas_call(
        _compress_kernel,
        grid=(2, batch),
        in_specs=[
            pl.BlockSpec((seq, LANES), lambda kv, b: (b, raw_block + kv)),
            pl.BlockSpec((None,) + pe.shape[1:], lambda kv, b: (kv, 0, 0)),
            pl.BlockSpec((None,) + w1e.shape[1:], lambda kv, b: (kv, 0, 0, 0)),
            pl.BlockSpec((None,) + w2e.shape[1:], lambda kv, b: (kv, 0, 0)),
            pl.BlockSpec((1, LANES), lambda kv, b: (0, 0)),
            pl.BlockSpec((n_chunk, LANES), lambda kv, b: (0, 0)),
            pl.BlockSpec((n_chunk, LANES), lambda kv, b: (0, 0)),
            pl.BlockSpec((2 * LANES, LANES), lambda kv, b: (0, 0)),
        ],
        out_specs=pl.BlockSpec((None, n_chunk, LANES), lambda kv, b: (kv, b, 0)),
        out_shape=jax.ShapeDtypeStruct((2, batch * n_chunk, LANES), BF16),
        compiler_params=_params("parallel", "parallel"),
        name="compress",
    )(fg, pe, w1e, w2e, gain, cos_c, sin_c, seg2)


def _nsa_kernel(q_ref, ke_ref, vsel_ref, kwin_ref, vwin_ref, kc_ref, vc_ref, gate_ref, wslct_ref, o_ref,
                *, n_cmp):
    qi = pl.program_id(1)
    q0 = qi * TQ
    rows = B_GQA * TQ
    lane = lax.broadcasted_iota(I32, (TQ, LANES), 1)
    low = lane < HEAD_DIM
    t1 = q0 + lax.broadcasted_iota(I32, (TQ, 1), 0)
    t4 = jnp.concatenate([t1] * B_GQA, axis=0)
    gates = jax.nn.sigmoid(gate_ref[...])
    n_ck = kc_ref.shape[0]
    cidx = lax.broadcasted_iota(I32, (rows, n_ck), 1)
    cmp_valid = (cidx * CMP_STRIDE + (CMP_LEN - 1) <= t4) & (cidx < n_cmp)
    blk = lax.broadcasted_iota(I32, (LANES, TQ), 0)
    tok = q0 + lax.broadcasted_iota(I32, (LANES, TQ), 1)
    cur = jnp.right_shift(tok, SEL_SHIFT)
    forced = (blk == 0) | (blk == cur) | (blk == cur - 1)
    causal_blk = blk * SEL_LEN <= tok
    blk_f = blk.astype(F32)

    def gate_col(branch, h):
        base = branch * B_Q_HEADS + h * B_GQA
        return jnp.concatenate([gates[:, base + g:base + g + 1] for g in range(B_GQA)], axis=0)

    qps, qaugs, o_tots = [], [], []
    for h in range(B_KV_HEADS):
        pieces = []
        for g in range(B_GQA):
            c, half = (h * B_GQA + g) // 2, g % 2
            chunk = q_ref[:, c * LANES:(c + 1) * LANES].astype(F32)
            if half != h:
                chunk = pltpu.roll(chunk, HEAD_DIM, 1)
            chunk = jnp.where(low if h == 0 else jnp.logical_not(low), chunk, 0.0)
            pieces.append((chunk * (HEAD_DIM ** -0.5)).astype(BF16))
        qp = jnp.concatenate(pieces, axis=0)
        qps.append(qp)

        s = jnp.where(cmp_valid, _dot_nt(qp, kc_ref[...]), NEG)
        m = jnp.max(s, axis=-1, keepdims=True)
        e = jnp.where(cmp_valid, jnp.exp(s - m), 0.0)
        p = e / jnp.maximum(jnp.sum(e, axis=-1, keepdims=True), TINY)
        o_tots.append(gate_col(0, h) * _dot(p.astype(BF16), vc_ref[...]))

        imp = p[0:TQ]
        for g in range(1, B_GQA):
            imp = imp + p[g * TQ:(g + 1) * TQ]
        p_slc = _dot_nt(wslct_ref[...], _split(imp))
        score = jnp.where(forced, FORCE_SCORE, jnp.where(causal_blk, p_slc, -1.0))
        chosen = jnp.zeros((LANES, TQ), F32)
        for _ in range(N_SELECT):
            best = jnp.max(score, axis=0, keepdims=True)
            first = jnp.min(jnp.where(score == best, blk_f, float(LANES)), axis=0, keepdims=True)
            pick = blk_f == first
            chosen = jnp.where(pick, 1.0, chosen)
            score = jnp.where(pick, -2.0, score)
        bias = jnp.where(chosen > 0.0, 0.0, NEG).T.astype(BF16)
        qaugs.append(jnp.concatenate([qp, jnp.concatenate([bias] * B_GQA, axis=0)], axis=1))

    key_col = lax.broadcasted_iota(I32, (rows, SEL_TK), 1)

    def sel_tile(kt, carry, causal):
        k0 = pl.multiple_of(kt * SEL_TK, SEL_TK)
        ke = ke_ref[pl.ds(k0, SEL_TK), :]
        vt = vsel_ref[pl.ds(k0, SEL_TK), :]
        out = []
        for h in range(B_KV_HEADS):
            m_i, l_i, acc = carry[h]
            s = _dot_nt(qaugs[h], ke)
            if causal:
                s = jnp.where(k0 + key_col <= t4, s, NEG)
            m_new = jnp.maximum(m_i, jnp.max(s, axis=-1, keepdims=True))
            alpha = jnp.exp(m_i - m_new)
            pe = jnp.exp(s - m_new)
            l_new = alpha * l_i + jnp.sum(pe, axis=-1, keepdims=True)
            out.append((m_new, l_new, alpha * acc + _dot(pe.astype(BF16), vt)))
        return tuple(out)

    init = tuple((jnp.full((rows, 1), NEG, F32), jnp.zeros((rows, 1), F32), jnp.zeros((rows, LANES), F32))
                 for _ in range(B_KV_HEADS))
    n_full = q0 // SEL_TK
    carry = lax.fori_loop(0, n_full, lambda kt, c: sel_tile(kt, c, False), init)
    carry = sel_tile(n_full, carry, True)

    span = WIN + TQ
    w0 = pl.multiple_of(jnp.maximum(q0 - WIN, 0), TQ)
    dist = t4 - (w0 + lax.broadcasted_iota(I32, (rows, span), 1))
    win_valid = (dist >= 0) & (dist < WIN)
    kw = kwin_ref[pl.ds(w0, span), :]
    vw = vwin_ref[pl.ds(w0, span), :]
    for h in range(B_KV_HEADS):
        _, l_s, acc_s = carry[h]
        o_tot = o_tots[h] + gate_col(1, h) * (acc_s / l_s)
        s = jnp.where(win_valid, _dot_nt(qps[h], kw), NEG)
        m = jnp.max(s, axis=-1, keepdims=True)
        e = jnp.exp(s - m)
        l_w = jnp.sum(e, axis=-1, keepdims=True)
        o_tot = o_tot + gate_col(2, h) * (_dot(e.astype(BF16), vw) / l_w)

        for c2 in range(B_GQA // 2):
            halves = []
            for half in range(2):
                g = 2 * c2 + half
                piece = o_tot[g * TQ:(g + 1) * TQ]
                halves.append(piece if half == h else pltpu.roll(piece, HEAD_DIM, 1))
            c = h * (B_GQA // 2) + c2
            o_ref[:, c * LANES:(c + 1) * LANES] = jnp.where(low, halves[0], halves[1]).astype(o_ref.dtype)


def _nsa(bqk, ke, bv, cmp_kv, fgate, wslct2, batch, seq, gate_block):
    t = bqk.shape[0]
    nq = seq // TQ
    n_ck = cmp_kv.shape[1] // batch
    n_cmp = seq // CMP_STRIDE - CMP_LEN // CMP_STRIDE + 1
    return pl.pallas_call(
        functools.partial(_nsa_kernel, n_cmp=n_cmp),
        grid=(batch, nq),
        in_specs=[
            pl.BlockSpec((TQ, B_QD), lambda b, i: (b * nq + i, 0)),
            pl.BlockSpec((seq, 2 * LANES), lambda b, i: (b, 0)),
            pl.BlockSpec((seq, LANES), lambda b, i: (b, 0)),
            pl.BlockSpec((seq, LANES), lambda b, i: (b, B_QD // LANES + 1)),
            pl.BlockSpec((seq, LANES), lambda b, i: (b, 1)),
            pl.BlockSpec((None, n_ck, LANES), lambda b, i: (0, b, 0)),
            pl.BlockSpec((None, n_ck, LANES), lambda b, i: (1, b, 0)),
            pl.BlockSpec((TQ, LANES), lambda b, i: (b * nq + i, gate_block)),
            pl.BlockSpec(wslct2.shape, lambda b, i: (0, 0)),
        ],
        out_specs=pl.BlockSpec((TQ, B_OUT), lambda b, i: (b * nq + i, 0)),
        out_shape=jax.ShapeDtypeStruct((t, B_OUT), BF16),
        compiler_params=_params("parallel", "parallel"),
        name="nsa_mixer",
    )(bqk, ke, bv, bqk, bv, cmp_kv, cmp_kv, fgate, wslct2)


def _merge_kernel(o0_ref, o1_ref, o2_ref, l0_ref, l1_ref, l2_ref, ob_ref, m0_ref, m1_ref, x_ref,
                  wpa_ref, wpb_ref, wout_ref, exp_ref, g_ref, xo_ref, h_ref, osc_ref, lsc_ref):
    def token_order(ref, scr, slot):
        d, n, width = ref.shape
        if d == 1:
            return ref[0]
        chunks = width // LANES
        for c in range(chunks):
            for r in range(d):
                scr[slot * chunks + c, pl.ds(r, n, stride=d), :] = ref[r, :, c * LANES:(c + 1) * LANES]
        return jnp.concatenate([scr[slot * chunks + c] for c in range(chunks)], axis=1)

    l0, l1, l2 = token_order(l0_ref, lsc_ref, 0), token_order(l1_ref, lsc_ref, 0), token_order(l2_ref, lsc_ref, 1)
    o0, o1, o2 = token_order(o0_ref, osc_ref, 0), token_order(o1_ref, osc_ref, 0), token_order(o2_ref, osc_ref, 1)
    mx = jnp.maximum(jnp.maximum(l0, l1), l2)
    e0, e1, e2 = jnp.exp(l0 - mx), jnp.exp(l1 - mx), jnp.exp(l2 - mx)
    den = e0 + e1 + e2
    ex = exp_ref[...]
    out_a = _split_dot(e0 / den, ex) * o0 + _split_dot(e1 / den, ex) * o1 + _split_dot(e2 / den, ex) * o2
    pa = _dot(out_a.astype(BF16), wpa_ref[...])
    pb = _dot(ob_ref[...], wpb_ref[...])
    mixed = jax.nn.sigmoid(m0_ref[...]) * pa + jax.nn.sigmoid(m1_ref[...]) * pb
    x = x_ref[...] + _dot(mixed.astype(BF16), wout_ref[...])
    xo_ref[...] = x
    y = x * lax.rsqrt(jnp.mean(x * x, axis=-1, keepdims=True) + EPS)
    h_ref[...] = (y * g_ref[...]).astype(h_ref.dtype)


def _merge(o_groups, lse_groups, out_b, fgate, x, wpa, wpb, wout, expand2, g_next, seq, tm=256):
    t, d = x.shape
    nseq = seq // tm
    row = lambda w: pl.BlockSpec((tm, w), lambda i: (i, 0))
    full = lambda a: pl.BlockSpec(a.shape, lambda i: (0,) * a.ndim)
    grouped = lambda a: pl.BlockSpec((None, a.shape[1], tm // a.shape[1], a.shape[3]),
                                     lambda i: (i // nseq, 0, i % nseq, 0))
    g_next = g_next.reshape(1, d)
    return pl.pallas_call(
        _merge_kernel,
        grid=(t // tm,),
        in_specs=[grouped(a) for a in o_groups] + [grouped(a) for a in lse_groups] + [
            row(B_OUT), pl.BlockSpec((tm, d), lambda i: (i, 0)), pl.BlockSpec((tm, d), lambda i: (i, 1)), row(d),
            full(wpa), full(wpb), full(wout), full(expand2), full(g_next)],
        out_specs=[row(d), row(d)],
        out_shape=[jax.ShapeDtypeStruct((t, d), F32), jax.ShapeDtypeStruct((t, d), BF16)],
        scratch_shapes=[pltpu.VMEM((2 * A_OUT // LANES, tm, LANES), F32), pltpu.VMEM((2, tm, LANES), F32)],
        compiler_params=_params("parallel"),
        name="merge_out",
    )(*o_groups, *lse_groups, out_b, fgate, fgate, x, wpa, wpb, wout, expand2, g_next)


def _ffn_up_kernel(h_ref, wg_ref, wu_ref, cw_ref, cb_ref, o_ref, gbuf_ref, *, tiles_per_seq):
    i = pl.program_id(1)
    tm = h_ref.shape[0]
    h = h_ref[...]

    @pl.when(i % tiles_per_seq == 0)
    def _():
        gbuf_ref[0:8, :] = jnp.zeros((8, gbuf_ref.shape[1]), F32)

    @pl.when(i % tiles_per_seq != 0)
    def _():
        gbuf_ref[0:8, :] = gbuf_ref[tm:tm + 8, :]

    gbuf_ref[8:tm + 8, :] = _dot(h, wg_ref[...])
    u = _dot(h, wu_ref[...])
    conv = cb_ref[...] + cw_ref[0:1, :] * gbuf_ref[pl.ds(6, tm), :]
    conv = conv + cw_ref[1:2, :] * gbuf_ref[pl.ds(7, tm), :]
    conv = conv + cw_ref[2:3, :] * gbuf_ref[pl.ds(8, tm), :]
    o_ref[...] = (jax.nn.silu(conv) * u).astype(o_ref.dtype)


def _ffn_up(h, w_up, conv_w, conv_b, seq, tm=512, tn=1408):
    t, d = h.shape
    nj = D_FF // tn
    return pl.pallas_call(
        functools.partial(_ffn_up_kernel, tiles_per_seq=seq // tm),
        grid=(nj, t // tm),
        in_specs=[
            pl.BlockSpec((tm, d), lambda j, i: (i, 0)),
            pl.BlockSpec((d, tn), lambda j, i: (0, j)),
            pl.BlockSpec((d, tn), lambda j, i: (0, nj + j)),
            pl.BlockSpec((CONV_W, tn), lambda j, i: (0, j)),
            pl.BlockSpec((1, tn), lambda j, i: (0, j)),
        ],
        out_specs=pl.BlockSpec((tm, tn), lambda j, i: (i, j)),
        out_shape=jax.ShapeDtypeStruct((t, D_FF), BF16),
        scratch_shapes=[pltpu.VMEM((tm + 8, tn), F32)],
        compiler_params=_params("arbitrary", "arbitrary"),
        name="ffn_up_conv",
    )(h, w_up, w_up, conv_w, conv_b.reshape(1, D_FF))


def _ffn_down_kernel(a_ref, w_ref, x_ref, g_ref, xo_ref, h_ref):
    x = x_ref[...] + _dot(a_ref[...], w_ref[...])
    xo_ref[...] = x
    y = x * lax.rsqrt(jnp.mean(x * x, axis=-1, keepdims=True) + EPS)
    h_ref[...] = (y * g_ref[...]).astype(h_ref.dtype)


def _ffn_down(act, w_down, x, g_next, tm=512):
    t, d = x.shape
    row = lambda w: pl.BlockSpec((tm, w), lambda i: (i, 0))
    return pl.pallas_call(
        _ffn_down_kernel,
        grid=(t // tm,),
        in_specs=[row(D_FF), pl.BlockSpec(w_down.shape, lambda i: (0, 0)), row(d),
                  pl.BlockSpec((1, d), lambda i: (0, 0))],
        out_specs=[row(d), row(d)],
        out_shape=[jax.ShapeDtypeStruct((t, d), F32), jax.ShapeDtypeStruct((t, d), BF16)],
        compiler_params=_params("parallel"),
        name="ffn_down",
    )(act, w_down, x, g_next.reshape(1, d))


def _rope_tables(pos):
    half = HEAD_DIM // 2
    inv_freq = ROPE_THETA ** (-jnp.arange(half, dtype=F32) / half)
    ang = pos.astype(F32)[:, None] * inv_freq[None, :]
    c, s = jnp.cos(ang), jnp.sin(ang)
    return jnp.tile(c, (1, 4)), jnp.tile(jnp.concatenate([-s, s], axis=1), (1, 2))


def _constants(n_ck, seq):
    lane = np.arange(LANES)
    seg = (lane[:, None] // HEAD_DIM == lane[None, :] // HEAD_DIM).astype(np.float32)
    seg2 = np.concatenate([seg, seg], axis=0)
    ratio = SEL_LEN // CMP_STRIDE
    wslc = np.zeros((n_ck, LANES), np.float32)
    for off in range(CMP_LEN // CMP_STRIDE):
        for i in range(n_ck):
            if i + off < n_ck:
                wslc[i, (i + off) // ratio] += 1.0
    wslct2 = np.concatenate([wslc, wslc], axis=0).T
    expand = np.zeros((LANES, A_OUT), np.float32)
    for j in range(A_HEADS_PER_GROUP):
        expand[j, j * HEAD_DIM:(j + 1) * HEAD_DIM] = 1.0
    expand2 = np.concatenate([expand, expand], axis=0)
    onehot = (np.arange(seq)[:, None] // SEL_LEN == lane[None, :]).astype(np.float32)
    return (jnp.asarray(seg2, BF16), jnp.asarray(wslct2, BF16), jnp.asarray(expand2, BF16),
            jnp.asarray(onehot, BF16))


def kernel(x, norm_mix_g, w_in, a_q_g, a_k_g, b_q_g, b_k_g, cmp_pos, cmp_w1, cmp_w2, w_proj_a, w_proj_b,
           w_out, norm_ffn_g, w_up, conv_w, conv_b, w_down):
    batch, seq, d = x.shape
    depth = w_in.shape[0]
    t = batch * seq
    assert d == D_MODEL and seq % (16 * TQ) == 0 and seq // SEL_LEN <= LANES
    n_ck = seq // CMP_STRIDE
    seg2, wslct2, expand2, onehot = _constants(n_ck, seq)
    onehot = jnp.tile(onehot, (batch, 1))
    cos_t, sin_t = _rope_tables(jnp.arange(seq))
    cos_c, sin_c = _rope_tables(jnp.arange(n_ck) * CMP_STRIDE + CMP_LEN - 1)
    eye_h = jnp.eye(B_KV_HEADS, dtype=F32)
    o_kq, o_kv, o_gate, o_merge = A_QKV, A_QKV + B_QD, A_QKV + B_QD + B_KV, A_QKV + B_QD + B_KV + B_GATE
    n_a = A_HEADS * HEAD_DIM
    f_gate_block = 2 * D_MODEL // LANES

    xf = x.reshape(t, d)
    h = _rmsnorm(xf, norm_mix_g[0])
    for l in range(depth):
        w = w_in[l]
        kv = lambda i: w[:, o_kv + i * LANES:o_kv + (i + 1) * LANES]
        w_bqk = jnp.concatenate([w[:, o_kq:o_kv], kv(2), kv(4)], axis=1).astype(BF16)
        w_bv = jnp.concatenate([kv(3), kv(5)], axis=1).astype(BF16)
        w_f = jnp.concatenate([w[:, o_merge:], w[:, o_gate:o_merge],
                               jnp.zeros((d, LANES - B_GATE), F32), kv(0), kv(1)], axis=1).astype(BF16)
        gain_a = jnp.concatenate([jnp.tile(a_q_g[l], A_HEADS_PER_GROUP),
                                  jnp.tile(a_k_g[l], A_HEADS_PER_GROUP)]).reshape(1, -1)
        gain_b = jnp.concatenate([jnp.tile(b_q_g[l], B_Q_HEADS), jnp.tile(b_k_g[l, 1], B_KV_HEADS),
                                  jnp.tile(b_k_g[l, 2], B_KV_HEADS)]).reshape(1, -1)

        o_groups, lse_groups = [], []
        for gi, (_, dil) in enumerate(DIL_PAIRS):
            cols = [w[:, part * n_a + gi * A_OUT:part * n_a + (gi + 1) * A_OUT] for part in range(3)]
            qkv = _proj_group(h, jnp.concatenate(cols, axis=1).astype(BF16), gain_a, cos_t, sin_t, seg2,
                              dil, batch, seq)
            o_g, lse_g = _band_attention(qkv, dil, batch, seq)
            o_groups.append(o_g)
            lse_groups.append(lse_g)

        bqk = _proj(h, w_bqk, BF16, tn=768, rope=(gain_b, cos_t, sin_t, seg2), seq=seq)
        bv = _proj(h, w_bv, BF16, tn=256)
        fg = _proj(h, w_f, F32, tn=w_f.shape[1], tm=256)
        pe = jnp.tile(cmp_pos[l], (1, 1, B_KV_HEADS))
        w1 = cmp_w1[l].reshape(2, CMP_LEN, HEAD_DIM, CMP_HIDDEN)
        w1e = jnp.einsum('kpdn,hj->kphdjn', w1, eye_h).reshape(2, CMP_LEN, LANES, 2 * CMP_HIDDEN)
        w2e = jnp.einsum('knd,hj->khnjd', cmp_w2[l], eye_h).reshape(2, 2 * CMP_HIDDEN, LANES)
        gain_c = jnp.tile(b_k_g[l, 0], B_KV_HEADS).reshape(1, LANES)
        cmp_kv = _compress(fg, f_gate_block + 1, pe, w1e.astype(BF16), w2e.astype(BF16), gain_c,
                           cos_c, sin_c, seg2, batch, seq)
        ke = jnp.concatenate([bqk[:, B_QD:B_QD + LANES], onehot], axis=1)
        out_b = _nsa(bqk, ke, bv, cmp_kv, fg, wslct2, batch, seq, f_gate_block)

        xf, h = _merge(o_groups, lse_groups, out_b, fg, xf, w_proj_a[l].astype(BF16), w_proj_b[l].astype(BF16),
                       w_out[l].astype(BF16), expand2, norm_ffn_g[l], seq)
        act = _ffn_up(h, w_up[l].astype(BF16), conv_w[l], conv_b[l], seq)
        g_next = norm_mix_g[l + 1] if l + 1 < depth else jnp.ones((d,), F32)
        xf, h = _ffn_down(act, w_down[l].astype(BF16), xf, g_next)
    return xf.reshape(batch, seq, d)
```

```python
import functools

import numpy as np
import jax
import jax.numpy as jnp
from jax import lax
from jax.experimental import pallas as pl
from jax.experimental.pallas import tpu as pltpu

F32 = jnp.float32
BF16 = jnp.bfloat16
I32 = jnp.int32

D_MODEL = 1024
HEAD_DIM = 64
ROPE_THETA = 10000.0
EPS = 1e-6
NEG = -1e30
TINY = 1e-30
FORCE_SCORE = 1e9

DIL_PAIRS = ((128, 1), (512, 4), (2048, 16))
A_HEADS_PER_GROUP = 8
A_HEADS = 24
A_OUT = 512
A_BAND = 128

B_Q_HEADS = 8
B_KV_HEADS = 2
B_GQA = 4
CMP_LEN = 32
CMP_STRIDE = 16
CMP_HIDDEN = 256
SEL_LEN = 64
SEL_SHIFT = 6
N_SELECT = 16
WIN = 512
B_OUT = 512

D_FF = 2816
CONV_W = 3

A_QKV = 3 * A_HEADS * HEAD_DIM
B_QD = B_Q_HEADS * HEAD_DIM
B_KV = 3 * 2 * B_KV_HEADS * HEAD_DIM
B_GATE = 3 * B_Q_HEADS

LOG2E = 1.4426950408889634
Q_SCALE = LOG2E * HEAD_DIM ** -0.5

LANES = 128
TQ = 128
SEL_TK = 512
CH = 32
VMEM_LIMIT = 48 * 1024 * 1024

_NT = (((1,), (1,)), ((), ()))


def _dot(a, b):
    return jnp.dot(a, b, preferred_element_type=F32)


def _dot_nt(a, b):
    return lax.dot_general(a, b, _NT, preferred_element_type=F32)


def _split(x):
    hi = x.astype(BF16)
    lo = (x - hi.astype(F32)).astype(BF16)
    return jnp.concatenate([hi, lo], axis=1)


def _split_dot(x, w2):
    return _dot(_split(x), w2)


def _params(*sem):
    return pltpu.CompilerParams(dimension_semantics=sem, vmem_limit_bytes=VMEM_LIMIT)


def _rmsnorm_kernel(x_ref, g_ref, h_ref):
    x = x_ref[...]
    y = x * lax.rsqrt(jnp.mean(x * x, axis=-1, keepdims=True) + EPS)
    h_ref[...] = (y * g_ref[...]).astype(h_ref.dtype)


def _rmsnorm(x, g, tm=512):
    t, d = x.shape
    return pl.pallas_call(
        _rmsnorm_kernel,
        grid=(t // tm,),
        in_specs=[pl.BlockSpec((tm, d), lambda i: (i, 0)), pl.BlockSpec((1, d), lambda i: (0, 0))],
        out_specs=pl.BlockSpec((tm, d), lambda i: (i, 0)),
        out_shape=jax.ShapeDtypeStruct((t, d), BF16),
        compiler_params=_params("parallel"),
        name="rmsnorm",
    )(x, g.reshape(1, d))


def _head_norm_rope(y, gain, cos_t, sin_t, seg2):
    ss = _split_dot(y * y, seg2)
    yn = y * lax.rsqrt(ss * (1.0 / HEAD_DIM) + EPS) * gain
    lane = lax.broadcasted_iota(I32, yn.shape, 1)
    partner = jnp.where((lane & 32) == 0, pltpu.roll(yn, 96, 1), pltpu.roll(yn, 32, 1))
    return yn * cos_t + partner * sin_t


def _proj_kernel(h_ref, w_ref, o_ref):
    o_ref[...] = _dot(h_ref[...], w_ref[...]).astype(o_ref.dtype)


def _proj_rope_kernel(h_ref, w_ref, gain_ref, cos_ref, sin_ref, seg_ref, o_ref):
    y = _dot(h_ref[...], w_ref[...])
    cos_t, sin_t, seg2 = cos_ref[...], sin_ref[...], seg_ref[...]
    for c in range(y.shape[1] // LANES):
        sl = slice(c * LANES, (c + 1) * LANES)
        o_ref[:, sl] = _head_norm_rope(y[:, sl], gain_ref[:, sl], cos_t, sin_t, seg2).astype(o_ref.dtype)


def _proj(h, w, out_dtype, tn, tm=512, rope=None, seq=None):
    t, k = h.shape
    n = w.shape[1]
    grid = (n // tn, t // tm)
    h_spec = pl.BlockSpec((tm, k), lambda j, i: (i, 0))
    w_spec = pl.BlockSpec((k, tn), lambda j, i: (0, j))
    o_spec = pl.BlockSpec((tm, tn), lambda j, i: (i, j))
    out_shape = jax.ShapeDtypeStruct((t, n), out_dtype)
    if rope is None:
        return pl.pallas_call(_proj_kernel, grid=grid, in_specs=[h_spec, w_spec], out_specs=o_spec,
                              out_shape=out_shape, compiler_params=_params("parallel", "parallel"),
                              name="proj")(h, w)
    gain, cos_t, sin_t, seg2 = rope
    nseq = seq // tm
    return pl.pallas_call(
        _proj_rope_kernel, grid=grid,
        in_specs=[h_spec, w_spec,
                  pl.BlockSpec((1, tn), lambda j, i: (0, j)),
                  pl.BlockSpec((tm, LANES), lambda j, i: (i % nseq, 0)),
                  pl.BlockSpec((tm, LANES), lambda j, i: (i % nseq, 0)),
                  pl.BlockSpec((2 * LANES, LANES), lambda j, i: (0, 0))],
        out_specs=o_spec, out_shape=out_shape,
        compiler_params=_params("parallel", "parallel"), name="proj_rope",
    )(h, w, gain, cos_t, sin_t, seg2)


def _proj_group_kernel(h_ref, w_ref, gain_ref, cos_ref, sin_ref, seg_ref, o_ref, y_ref, *, n_rope):
    dil, n_sub, _ = o_ref.shape
    y = _dot(h_ref[...], w_ref[...])
    cos_t, sin_t, seg2 = cos_ref[...], sin_ref[...], seg_ref[...]
    for c in range(y.shape[1] // LANES):
        sl = slice(c * LANES, (c + 1) * LANES)
        val = _head_norm_rope(y[:, sl], gain_ref[:, sl], cos_t, sin_t, seg2) if c < n_rope else y[:, sl]
        if dil == 1:
            o_ref[0, :, sl] = val.astype(o_ref.dtype)
        else:
            y_ref[c] = val
            for r in range(dil):
                o_ref[r, :, sl] = y_ref[c, pl.ds(r, n_sub, stride=dil), :].astype(o_ref.dtype)


def _proj_group(h, w, gain, cos_t, sin_t, seg2, dil, batch, seq, tm=512):
    t, k = h.shape
    n = w.shape[1]
    nseq = seq // tm
    return pl.pallas_call(
        functools.partial(_proj_group_kernel, n_rope=gain.shape[1] // LANES),
        grid=(t // tm,),
        in_specs=[pl.BlockSpec((tm, k), lambda i: (i, 0)),
                  pl.BlockSpec((k, n), lambda i: (0, 0)),
                  pl.BlockSpec(gain.shape, lambda i: (0, 0)),
                  pl.BlockSpec((tm, LANES), lambda i: (i % nseq, 0)),
                  pl.BlockSpec((tm, LANES), lambda i: (i % nseq, 0)),
                  pl.BlockSpec((2 * LANES, LANES), lambda i: (0, 0))],
        out_specs=pl.BlockSpec((None, dil, tm // dil, n), lambda i: (i // nseq, 0, i % nseq, 0)),
        out_shape=jax.ShapeDtypeStruct((batch, dil, seq // dil, n), BF16),
        scratch_shapes=[pltpu.VMEM((n // LANES, tm, LANES), F32)],
        compiler_params=_params("parallel"),
        name=f"proj_group_d{dil}",
    )(h, w, gain, cos_t, sin_t, seg2)


def _band_kernel(q_ref, kp_ref, kc_ref, vp_ref, vc_ref, o_ref, lse_ref):
    i = pl.program_id(1)
    row = lax.broadcasted_iota(I32, (2 * TQ, 2 * TQ), 0) & (TQ - 1)
    col = lax.broadcasted_iota(I32, (2 * TQ, 2 * TQ), 1)
    dist = TQ + row - col
    valid = (dist >= 0) & (dist <= A_BAND) & ((col >= TQ) | (i > 0))
    lane = lax.broadcasted_iota(I32, (TQ, LANES), 1)
    low = lane < HEAD_DIM
    lse_tile = jnp.zeros((TQ, LANES), F32)
    for p in range(A_HEADS_PER_GROUP // 2):
        sl = slice(p * LANES, (p + 1) * LANES)
        q2 = q_ref[:, sl].astype(F32)
        k2 = jnp.concatenate([kp_ref[:, sl], kc_ref[:, sl]], axis=0)
        v2 = jnp.concatenate([vp_ref[:, sl], vc_ref[:, sl]], axis=0)
        qs = jnp.concatenate([jnp.where(low, q2, 0.0), jnp.where(low, 0.0, q2)], axis=0).astype(BF16)
        s = jnp.where(valid, _dot_nt(qs, k2), NEG)
        m = jnp.max(s, axis=-1, keepdims=True)
        e = jnp.exp2(s - m)
        l = jnp.sum(e, axis=-1, keepdims=True)
        o2 = _dot(e.astype(BF16), v2) / l
        lse2 = m + jnp.log(l) * LOG2E
        lse_tile = jnp.where(lane == 2 * p, lse2[:TQ], jnp.where(lane == 2 * p + 1, lse2[TQ:], lse_tile))
        o_ref[:, sl] = jnp.where(low, o2[:TQ], o2[TQ:])
    lse_ref[...] = lse_tile


def _band_attention(qkv, dil, batch, seq):
    sub = seq // dil
    nb = sub // TQ
    n_sub = batch * dil
    flat = qkv.reshape(n_sub * sub, qkv.shape[-1])

    def cur(n, i):
        return n * nb + i

    def prev(n, i):
        return n * nb + jnp.maximum(i - 1, 0)

    o, lse = pl.pallas_call(
        _band_kernel,
        grid=(n_sub, nb),
        in_specs=[
            pl.BlockSpec((TQ, A_OUT), lambda n, i: (cur(n, i), 0)),
            pl.BlockSpec((TQ, A_OUT), lambda n, i: (prev(n, i), 1)),
            pl.BlockSpec((TQ, A_OUT), lambda n, i: (cur(n, i), 1)),
            pl.BlockSpec((TQ, A_OUT), lambda n, i: (prev(n, i), 2)),
            pl.BlockSpec((TQ, A_OUT), lambda n, i: (cur(n, i), 2)),
        ],
        out_specs=[pl.BlockSpec((TQ, A_OUT), lambda n, i: (cur(n, i), 0)),
                   pl.BlockSpec((TQ, LANES), lambda n, i: (cur(n, i), 0))],
        out_shape=[jax.ShapeDtypeStruct((n_sub * sub, A_OUT), F32),
                   jax.ShapeDtypeStruct((n_sub * sub, LANES), F32)],
        compiler_params=_params("parallel", "parallel"),
        name=f"band_attention_d{dil}",
    )(flat, flat, flat, flat, flat)
    return o.reshape(batch, dil, sub, A_OUT), lse.reshape(batch, dil, sub, LANES)


def _compress_kernel(x_ref, pe_ref, w1_ref, w2_ref, gain_ref, cos_ref, sin_ref, seg_ref, o_ref):
    kv = pl.program_id(0)
    n = o_ref.shape[0]
    top = jnp.zeros((n, w1_ref.shape[-1]), F32)
    bot = jnp.zeros((n, w1_ref.shape[-1]), F32)
    for p in range(CMP_STRIDE):
        xp = x_ref[pl.ds(p, n, stride=CMP_STRIDE), :]
        top = top + _dot((xp + pe_ref[p:p + 1, :]).astype(BF16), w1_ref[p])
        bot = bot + _dot((xp + pe_ref[CMP_STRIDE + p:CMP_STRIDE + p + 1, :]).astype(BF16), w1_ref[CMP_STRIDE + p])
    hid = jax.nn.gelu(top + pltpu.roll(bot, n - 1, 0))
    out = _dot(hid.astype(BF16), w2_ref[...])

    @pl.when(kv == 0)
    def _():
        o_ref[...] = _head_norm_rope(out, gain_ref[...], cos_ref[...], sin_ref[...], seg_ref[...]).astype(o_ref.dtype)

    @pl.when(kv != 0)
    def _():
        o_ref[...] = out.astype(o_ref.dtype)


def _compress(fg, raw_block, pe, w1e, w2e, gain, cos_c, sin_c, seg2, batch, seq):
    n_chunk = seq // CMP_STRIDE
    return pl.pallas_call(
        _compress_kernel,
        grid=(2, batch),
        in_specs=[
            pl.BlockSpec((seq, LANES), lambda kv, b: (b, raw_block + kv)),
            pl.BlockSpec((None,) + pe.shape[1:], lambda kv, b: (kv, 0, 0)),
            pl.BlockSpec((None,) + w1e.shape[1:], lambda kv, b: (kv, 0, 0, 0)),
            pl.BlockSpec((None,) + w2e.shape[1:], lambda kv, b: (kv, 0, 0)),
            pl.BlockSpec((1, LANES), lambda kv, b: (0, 0)),
            pl.BlockSpec((n_chunk, LANES), lambda kv, b: (0, 0)),
            pl.BlockSpec((n_chunk, LANES), lambda kv, b: (0, 0)),
            pl.BlockSpec((2 * LANES, LANES), lambda kv, b: (0, 0)),
        ],
        out_specs=pl.BlockSpec((None, n_chunk, LANES), lambda kv, b: (kv, b, 0)),
        out_shape=jax.ShapeDtypeStruct((2, batch * n_chunk, LANES), BF16),
        compiler_params=_params("parallel", "parallel"),
        name="compress",
    )(fg, pe, w1e, w2e, gain, cos_c, sin_c, seg2)


def _nsa_kernel(q_ref, ke_ref, vsel_ref, kwin_ref, vwin_ref, kc_ref, vc_ref, gate_ref, wslct_ref, o_ref,
                qa0, qa1, s0, s1, p0, p1, st0, st1, acc0, acc1, ot0, ot1, bias_ref, cmask_ref, imp_ref,
                *, n_cmp):
    qa, s_scr, p_scr, st = (qa0, qa1), (s0, s1), (p0, p1), (st0, st1)
    acc, ot = (acc0, acc1), (ot0, ot1)
    m_row, l_row, a_row = 0, 1, 2
    qi = pl.program_id(1)
    q0 = qi * TQ
    rows = B_GQA * TQ
    n_ck = kc_ref.shape[0]
    span = WIN + TQ
    n_full = q0 // SEL_TK
    w0 = pl.multiple_of(jnp.maximum(q0 - WIN, 0), TQ)
    lane = lax.broadcasted_iota(I32, (TQ, LANES), 1)
    low = lane < HEAD_DIM
    gates = jax.nn.sigmoid(gate_ref[...])

    def token_key(width):
        return (q0 + lax.broadcasted_iota(I32, (TQ, width), 0), lax.broadcasted_iota(I32, (TQ, width), 1))

    tr, cc = token_key(n_ck)
    cmp_valid = (cc * CMP_STRIDE + (CMP_LEN - 1) <= tr) & (cc < n_cmp)
    bias_ref[0, :, :n_ck] = jnp.where(cmp_valid, 0.0, NEG)
    cmask_ref[...] = jnp.where(cmp_valid, 1.0, 0.0)
    tr, cc = token_key(SEL_TK)
    bias_ref[1, :, :SEL_TK] = jnp.where(n_full * SEL_TK + cc <= tr, 0.0, NEG)
    tr, cc = token_key(span)
    dist = tr - (w0 + cc)
    bias_ref[2, :, :span] = jnp.where((dist >= 0) & (dist < WIN), 0.0, NEG)

    blk = lax.broadcasted_iota(I32, (LANES, TQ), 0)
    tok = q0 + lax.broadcasted_iota(I32, (LANES, TQ), 1)
    cur = jnp.right_shift(tok, SEL_SHIFT)
    forced = (blk == 0) | (blk == cur) | (blk == cur - 1)
    causal_blk = blk * SEL_LEN <= tok
    blk_f = blk.astype(F32)

    def gate_col(branch, h):
        base = branch * B_Q_HEADS + h * B_GQA
        return jnp.concatenate([gates[:, base + g:base + g + 1] for g in range(B_GQA)], axis=0)

    def lanes(c):
        return slice(c * LANES, (c + 1) * LANES)

    def score_cols(h, r0, width, mask):
        ts = r0 % TQ
        cols = [s_scr[h][r0:r0 + CH, lanes(c)] for c in range(width // LANES)]
        if mask is not None:
            cols = [x + bias_ref[mask, ts:ts + CH, lanes(c)] for c, x in enumerate(cols)]
        return cols

    def row_max(cols):
        mx = cols[0]
        for x in cols[1:]:
            mx = jnp.maximum(mx, x)
        return jnp.max(mx, axis=-1, keepdims=True)

    def row_sum(cols):
        tot = cols[0]
        for x in cols[1:]:
            tot = tot + x
        return jnp.sum(tot, axis=-1, keepdims=True)

    def softmax_cmp(h):
        for r0 in range(0, rows, CH):
            ts = r0 % TQ
            cols = score_cols(h, r0, n_ck, 0)
            m = row_max(cols)
            es = [jnp.exp2(x - m) * cmask_ref[ts:ts + CH, lanes(c)] for c, x in enumerate(cols)]
            rinv = 1.0 / jnp.maximum(row_sum(es), TINY)
            for c, e in enumerate(es):
                pn = e * rinv
                p_scr[h][r0:r0 + CH, lanes(c)] = pn.astype(BF16)
                if r0 < TQ:
                    imp_ref[h, ts:ts + CH, lanes(c)] = pn
                else:
                    imp_ref[h, ts:ts + CH, lanes(c)] += pn

    def softmax_online(h, mask):
        for r0 in range(0, rows, CH):
            rs = slice(r0, r0 + CH)
            cols = score_cols(h, r0, SEL_TK, mask)
            m_old = st[h][m_row, rs, :]
            m_new = jnp.maximum(m_old, row_max(cols))
            alpha = jnp.exp2(m_old - m_new)
            ps = [jnp.exp2(x - m_new) for x in cols]
            for c, pc in enumerate(ps):
                p_scr[h][rs, lanes(c)] = pc.astype(BF16)
            st[h][l_row, rs, :] = alpha * st[h][l_row, rs, :] + row_sum(ps)
            st[h][m_row, rs, :] = m_new
            st[h][a_row, rs, :] = alpha

    def softmax_once(h, width, mask):
        for r0 in range(0, rows, CH):
            rs = slice(r0, r0 + CH)
            cols = score_cols(h, r0, width, mask)
            m = row_max(cols)
            ps = [jnp.exp2(x - m) for x in cols]
            for c, pc in enumerate(ps):
                p_scr[h][rs, lanes(c)] = pc.astype(BF16)
            st[h][l_row, rs, :] = jnp.broadcast_to(row_sum(ps), (CH, LANES))

    def score_tile(h, kt):
        k0 = pl.multiple_of(kt * SEL_TK, SEL_TK)
        s_scr[h][:, :SEL_TK] = _dot_nt(qa[h][...], ke_ref[pl.ds(k0, SEL_TK), :])

    def value_tile(h, kt, live=None):
        k0 = pl.multiple_of(kt * SEL_TK, SEL_TK)
        pv = _dot(p_scr[h][:, :SEL_TK], vsel_ref[pl.ds(k0, SEL_TK), :])
        if live is not None:
            pv = jnp.where(live, pv, 0.0)
        acc[h][...] = st[h][a_row] * acc[h][...] + pv

    for h in range(B_KV_HEADS):
        for g in range(B_GQA):
            c, half = (h * B_GQA + g) // 2, g % 2
            chunk = q_ref[:, lanes(c)].astype(F32)
            if half != h:
                chunk = pltpu.roll(chunk, HEAD_DIM, 1)
            chunk = jnp.where(low if h == 0 else jnp.logical_not(low), chunk, 0.0)
            qa[h][g * TQ:(g + 1) * TQ, :LANES] = chunk.astype(BF16)
        s_scr[h][:, :n_ck] = _dot_nt(qa[h][:, :LANES], kc_ref[...])

    for h in range(B_KV_HEADS):
        softmax_cmp(h)
        ot[h][...] = gate_col(0, h) * _dot(p_scr[h][:, :n_ck], vc_ref[...])

        p_slc = _dot_nt(wslct_ref[...], _split(imp_ref[h]))
        score = jnp.where(forced, FORCE_SCORE, jnp.where(causal_blk, p_slc, -1.0))
        chosen = jnp.zeros((LANES, TQ), F32)
        for _ in range(N_SELECT):
            best = jnp.max(score, axis=0, keepdims=True)
            first = jnp.min(jnp.where(score == best, blk_f, float(LANES)), axis=0, keepdims=True)
            pick = blk_f == first
            chosen = jnp.where(pick, 1.0, chosen)
            score = jnp.where(pick, -2.0, score)
        bias = jnp.where(chosen > 0.0, 0.0, NEG).T.astype(BF16)
        qa[h][:, LANES:] = jnp.concatenate([bias] * B_GQA, axis=0)

        st[h][m_row] = jnp.full((rows, LANES), NEG, F32)
        st[h][l_row] = jnp.zeros((rows, LANES), F32)
        st[h][a_row] = jnp.ones((rows, LANES), F32)
        acc[h][...] = jnp.zeros((rows, LANES), F32)

    score_tile(0, 0)

    def sweep(kt, carry):
        value_tile(1, jnp.maximum(kt - 1, 0), live=kt > 0)
        score_tile(1, kt)
        softmax_online(0, None)
        value_tile(0, kt)
        score_tile(0, kt + 1)
        softmax_online(1, None)
        return carry

    lax.fori_loop(0, n_full, sweep, 0)
    value_tile(1, jnp.maximum(n_full - 1, 0), live=n_full > 0)
    score_tile(1, n_full)
    softmax_online(0, 1)
    value_tile(0, n_full)
    softmax_online(1, 1)
    value_tile(1, n_full)

    kw = kwin_ref[pl.ds(w0, span), :]
    vw = vwin_ref[pl.ds(w0, span), :]
    for h in range(B_KV_HEADS):
        acc[h][...] = acc[h][...] / st[h][l_row]
        s_scr[h][:, :span] = _dot_nt(qa[h][:, :LANES], kw)
    for h in range(B_KV_HEADS):
        softmax_once(h, span, 2)
        o_win = _dot(p_scr[h][:, :span], vw) / st[h][l_row]
        o_tot = ot[h][...] + gate_col(1, h) * acc[h][...] + gate_col(2, h) * o_win

        for c2 in range(B_GQA // 2):
            halves = []
            for half in range(2):
                g = 2 * c2 + half
                piece = o_tot[g * TQ:(g + 1) * TQ]
                halves.append(piece if half == h else pltpu.roll(piece, HEAD_DIM, 1))
            o_ref[:, lanes(h * (B_GQA // 2) + c2)] = jnp.where(low, halves[0], halves[1]).astype(o_ref.dtype)


def _nsa(bqk, ke, bv, cmp_kv, fgate, wslct2, batch, seq, gate_block):
    t = bqk.shape[0]
    nq = seq // TQ
    n_ck = cmp_kv.shape[1] // batch
    n_cmp = seq // CMP_STRIDE - CMP_LEN // CMP_STRIDE + 1
    rows = B_GQA * TQ
    width = max(WIN + TQ, SEL_TK, n_ck)
    return pl.pallas_call(
        functools.partial(_nsa_kernel, n_cmp=n_cmp),
        grid=(batch, nq),
        in_specs=[
            pl.BlockSpec((TQ, B_QD), lambda b, i: (b * nq + i, 0)),
            pl.BlockSpec((seq, 2 * LANES), lambda b, i: (b, 0)),
            pl.BlockSpec((seq, LANES), lambda b, i: (b, 0)),
            pl.BlockSpec((seq, LANES), lambda b, i: (b, B_QD // LANES + 1)),
            pl.BlockSpec((seq, LANES), lambda b, i: (b, 1)),
            pl.BlockSpec((None, n_ck, LANES), lambda b, i: (0, b, 0)),
            pl.BlockSpec((None, n_ck, LANES), lambda b, i: (1, b, 0)),
            pl.BlockSpec((TQ, LANES), lambda b, i: (b * nq + i, gate_block)),
            pl.BlockSpec(wslct2.shape, lambda b, i: (0, 0)),
        ],
        out_specs=pl.BlockSpec((TQ, B_OUT), lambda b, i: (b * nq + i, 0)),
        out_shape=jax.ShapeDtypeStruct((t, B_OUT), BF16),
        scratch_shapes=(
            [pltpu.VMEM((rows, 2 * LANES), BF16)] * 2 +
            [pltpu.VMEM((rows, width), F32)] * 2 +
            [pltpu.VMEM((rows, width), BF16)] * 2 +
            [pltpu.VMEM((3, rows, LANES), F32)] * 2 +
            [pltpu.VMEM((rows, LANES), F32)] * 4 +
            [pltpu.VMEM((3, TQ, width), F32), pltpu.VMEM((TQ, n_ck), F32), pltpu.VMEM((2, TQ, n_ck), F32)]),
        compiler_params=_params("parallel", "parallel"),
        name="nsa_mixer",
    )(bqk, ke, bv, bqk, bv, cmp_kv, cmp_kv, fgate, wslct2)


def _merge_kernel(o0_ref, o1_ref, o2_ref, l0_ref, l1_ref, l2_ref, ob_ref, m0_ref, m1_ref, x_ref,
                  wpa_ref, wpb_ref, wout_ref, exp_ref, g_ref, xo_ref, h_ref, osc_ref, lsc_ref):
    def token_order(ref, scr, slot):
        d, n, width = ref.shape
        if d == 1:
            return ref[0]
        chunks = width // LANES
        for c in range(chunks):
            for r in range(d):
                scr[slot * chunks + c, pl.ds(r, n, stride=d), :] = ref[r, :, c * LANES:(c + 1) * LANES]
        return jnp.concatenate([scr[slot * chunks + c] for c in range(chunks)], axis=1)

    l0, l1, l2 = token_order(l0_ref, lsc_ref, 0), token_order(l1_ref, lsc_ref, 0), token_order(l2_ref, lsc_ref, 1)
    o0, o1, o2 = token_order(o0_ref, osc_ref, 0), token_order(o1_ref, osc_ref, 0), token_order(o2_ref, osc_ref, 1)
    mx = jnp.maximum(jnp.maximum(l0, l1), l2)
    e0, e1, e2 = jnp.exp2(l0 - mx), jnp.exp2(l1 - mx), jnp.exp2(l2 - mx)
    den = e0 + e1 + e2
    ex = exp_ref[...]
    out_a = _split_dot(e0 / den, ex) * o0 + _split_dot(e1 / den, ex) * o1 + _split_dot(e2 / den, ex) * o2
    pa = _dot(out_a.astype(BF16), wpa_ref[...])
    pb = _dot(ob_ref[...], wpb_ref[...])
    mixed = jax.nn.sigmoid(m0_ref[...]) * pa + jax.nn.sigmoid(m1_ref[...]) * pb
    x = x_ref[...] + _dot(mixed.astype(BF16), wout_ref[...])
    xo_ref[...] = x
    y = x * lax.rsqrt(jnp.mean(x * x, axis=-1, keepdims=True) + EPS)
    h_ref[...] = (y * g_ref[...]).astype(h_ref.dtype)


def _merge(o_groups, lse_groups, out_b, fgate, x, wpa, wpb, wout, expand2, g_next, seq, tm=256):
    t, d = x.shape
    nseq = seq // tm
    row = lambda w: pl.BlockSpec((tm, w), lambda i: (i, 0))
    full = lambda a: pl.BlockSpec(a.shape, lambda i: (0,) * a.ndim)
    grouped = lambda a: pl.BlockSpec((None, a.shape[1], tm // a.shape[1], a.shape[3]),
                                     lambda i: (i // nseq, 0, i % nseq, 0))
    g_next = g_next.reshape(1, d)
    return pl.pallas_call(
        _merge_kernel,
        grid=(t // tm,),
        in_specs=[grouped(a) for a in o_groups] + [grouped(a) for a in lse_groups] + [
            row(B_OUT), pl.BlockSpec((tm, d), lambda i: (i, 0)), pl.BlockSpec((tm, d), lambda i: (i, 1)), row(d),
            full(wpa), full(wpb), full(wout), full(expand2), full(g_next)],
        out_specs=[row(d), row(d)],
        out_shape=[jax.ShapeDtypeStruct((t, d), F32), jax.ShapeDtypeStruct((t, d), BF16)],
        scratch_shapes=[pltpu.VMEM((2 * A_OUT // LANES, tm, LANES), F32), pltpu.VMEM((2, tm, LANES), F32)],
        compiler_params=_params("parallel"),
        name="merge_out",
    )(*o_groups, *lse_groups, out_b, fgate, fgate, x, wpa, wpb, wout, expand2, g_next)


def _ffn_up_kernel(h_ref, wg_ref, wu_ref, cw_ref, cb_ref, o_ref, gbuf_ref, *, tiles_per_seq):
    i = pl.program_id(1)
    tm = h_ref.shape[0]
    h = h_ref[...]

    @pl.when(i % tiles_per_seq == 0)
    def _():
        gbuf_ref[0:8, :] = jnp.zeros((8, gbuf_ref.shape[1]), F32)

    @pl.when(i % tiles_per_seq != 0)
    def _():
        gbuf_ref[0:8, :] = gbuf_ref[tm:tm + 8, :]

    gbuf_ref[8:tm + 8, :] = _dot(h, wg_ref[...])
    u = _dot(h, wu_ref[...])
    conv = cb_ref[...] + cw_ref[0:1, :] * gbuf_ref[pl.ds(6, tm), :]
    conv = conv + cw_ref[1:2, :] * gbuf_ref[pl.ds(7, tm), :]
    conv = conv + cw_ref[2:3, :] * gbuf_ref[pl.ds(8, tm), :]
    o_ref[...] = (jax.nn.silu(conv) * u).astype(o_ref.dtype)


def _ffn_up(h, w_up, conv_w, conv_b, seq, tm=512, tn=1408):
    t, d = h.shape
    nj = D_FF // tn
    return pl.pallas_call(
        functools.partial(_ffn_up_kernel, tiles_per_seq=seq // tm),
        grid=(nj, t // tm),
        in_specs=[
            pl.BlockSpec((tm, d), lambda j, i: (i, 0)),
            pl.BlockSpec((d, tn), lambda j, i: (0, j)),
            pl.BlockSpec((d, tn), lambda j, i: (0, nj + j)),
            pl.BlockSpec((CONV_W, tn), lambda j, i: (0, j)),
            pl.BlockSpec((1, tn), lambda j, i: (0, j)),
        ],
        out_specs=pl.BlockSpec((tm, tn), lambda j, i: (i, j)),
        out_shape=jax.ShapeDtypeStruct((t, D_FF), BF16),
        scratch_shapes=[pltpu.VMEM((tm + 8, tn), F32)],
        compiler_params=_params("arbitrary", "arbitrary"),
        name="ffn_up_conv",
    )(h, w_up, w_up, conv_w, conv_b.reshape(1, D_FF))


def _ffn_down_kernel(a_ref, w_ref, x_ref, g_ref, xo_ref, h_ref):
    x = x_ref[...] + _dot(a_ref[...], w_ref[...])
    xo_ref[...] = x
    y = x * lax.rsqrt(jnp.mean(x * x, axis=-1, keepdims=True) + EPS)
    h_ref[...] = (y * g_ref[...]).astype(h_ref.dtype)


def _ffn_down(act, w_down, x, g_next, tm=512):
    t, d = x.shape
    row = lambda w: pl.BlockSpec((tm, w), lambda i: (i, 0))
    return pl.pallas_call(
        _ffn_down_kernel,
        grid=(t // tm,),
        in_specs=[row(D_FF), pl.BlockSpec(w_down.shape, lambda i: (0, 0)), row(d),
                  pl.BlockSpec((1, d), lambda i: (0, 0))],
        out_specs=[row(d), row(d)],
        out_shape=[jax.ShapeDtypeStruct((t, d), F32), jax.ShapeDtypeStruct((t, d), BF16)],
        compiler_params=_params("parallel"),
        name="ffn_down",
    )(act, w_down, x, g_next.reshape(1, d))


def _rope_tables(pos):
    half = HEAD_DIM // 2
    inv_freq = ROPE_THETA ** (-jnp.arange(half, dtype=F32) / half)
    ang = pos.astype(F32)[:, None] * inv_freq[None, :]
    c, s = jnp.cos(ang), jnp.sin(ang)
    return jnp.tile(c, (1, 4)), jnp.tile(jnp.concatenate([-s, s], axis=1), (1, 2))


def _constants(n_ck, seq):
    lane = np.arange(LANES)
    seg = (lane[:, None] // HEAD_DIM == lane[None, :] // HEAD_DIM).astype(np.float32)
    seg2 = np.concatenate([seg, seg], axis=0)
    ratio = SEL_LEN // CMP_STRIDE
    wslc = np.zeros((n_ck, LANES), np.float32)
    for off in range(CMP_LEN // CMP_STRIDE):
        for i in range(n_ck):
            if i + off < n_ck:
                wslc[i, (i + off) // ratio] += 1.0
    wslct2 = np.concatenate([wslc, wslc], axis=0).T
    expand = np.zeros((LANES, A_OUT), np.float32)
    for j in range(A_HEADS_PER_GROUP):
        expand[j, j * HEAD_DIM:(j + 1) * HEAD_DIM] = 1.0
    expand2 = np.concatenate([expand, expand], axis=0)
    onehot = (np.arange(seq)[:, None] // SEL_LEN == lane[None, :]).astype(np.float32)
    return (jnp.asarray(seg2, BF16), jnp.asarray(wslct2, BF16), jnp.asarray(expand2, BF16),
            jnp.asarray(onehot, BF16))


def kernel(x, norm_mix_g, w_in, a_q_g, a_k_g, b_q_g, b_k_g, cmp_pos, cmp_w1, cmp_w2, w_proj_a, w_proj_b,
           w_out, norm_ffn_g, w_up, conv_w, conv_b, w_down):
    batch, seq, d = x.shape
    depth = w_in.shape[0]
    t = batch * seq
    assert d == D_MODEL and seq % (16 * TQ) == 0 and seq // SEL_LEN <= LANES
    n_ck = seq // CMP_STRIDE
    seg2, wslct2, expand2, onehot = _constants(n_ck, seq)
    onehot = jnp.tile(onehot, (batch, 1))
    cos_t, sin_t = _rope_tables(jnp.arange(seq))
    cos_c, sin_c = _rope_tables(jnp.arange(n_ck) * CMP_STRIDE + CMP_LEN - 1)
    eye_h = jnp.eye(B_KV_HEADS, dtype=F32)
    o_kq, o_kv, o_gate, o_merge = A_QKV, A_QKV + B_QD, A_QKV + B_QD + B_KV, A_QKV + B_QD + B_KV + B_GATE
    n_a = A_HEADS * HEAD_DIM
    f_gate_block = 2 * D_MODEL // LANES

    xf = x.reshape(t, d)
    h = _rmsnorm(xf, norm_mix_g[0])
    for l in range(depth):
        w = w_in[l]
        kv = lambda i: w[:, o_kv + i * LANES:o_kv + (i + 1) * LANES]
        w_bqk = jnp.concatenate([w[:, o_kq:o_kv], kv(2), kv(4)], axis=1).astype(BF16)
        w_bv = jnp.concatenate([kv(3), kv(5)], axis=1).astype(BF16)
        w_f = jnp.concatenate([w[:, o_merge:], w[:, o_gate:o_merge],
                               jnp.zeros((d, LANES - B_GATE), F32), kv(0), kv(1)], axis=1).astype(BF16)
        gain_a = jnp.concatenate([jnp.tile(a_q_g[l] * Q_SCALE, A_HEADS_PER_GROUP),
                                  jnp.tile(a_k_g[l], A_HEADS_PER_GROUP)]).reshape(1, -1)
        gain_b = jnp.concatenate([jnp.tile(b_q_g[l] * Q_SCALE, B_Q_HEADS), jnp.tile(b_k_g[l, 1], B_KV_HEADS),
                                  jnp.tile(b_k_g[l, 2], B_KV_HEADS)]).reshape(1, -1)

        o_groups, lse_groups = [], []
        for gi, (_, dil) in enumerate(DIL_PAIRS):
            cols = [w[:, part * n_a + gi * A_OUT:part * n_a + (gi + 1) * A_OUT] for part in range(3)]
            qkv = _proj_group(h, jnp.concatenate(cols, axis=1).astype(BF16), gain_a, cos_t, sin_t, seg2,
                              dil, batch, seq)
            o_g, lse_g = _band_attention(qkv, dil, batch, seq)
            o_groups.append(o_g)
            lse_groups.append(lse_g)

        bqk = _proj(h, w_bqk, BF16, tn=768, rope=(gain_b, cos_t, sin_t, seg2), seq=seq)
        bv = _proj(h, w_bv, BF16, tn=256)
        fg = _proj(h, w_f, F32, tn=w_f.shape[1], tm=256)
        pe = jnp.tile(cmp_pos[l], (1, 1, B_KV_HEADS))
        w1 = cmp_w1[l].reshape(2, CMP_LEN, HEAD_DIM, CMP_HIDDEN)
        w1e = jnp.einsum('kpdn,hj->kphdjn', w1, eye_h).reshape(2, CMP_LEN, LANES, 2 * CMP_HIDDEN)
        w2e = jnp.einsum('knd,hj->khnjd', cmp_w2[l], eye_h).reshape(2, 2 * CMP_HIDDEN, LANES)
        gain_c = jnp.tile(b_k_g[l, 0], B_KV_HEADS).reshape(1, LANES)
        cmp_kv = _compress(fg, f_gate_block + 1, pe, w1e.astype(BF16), w2e.astype(BF16), gain_c,
                           cos_c, sin_c, seg2, batch, seq)
        ke = jnp.concatenate([bqk[:, B_QD:B_QD + LANES], onehot], axis=1)
        out_b = _nsa(bqk, ke, bv, cmp_kv, fg, wslct2, batch, seq, f_gate_block)

        xf, h = _merge(o_groups, lse_groups, out_b, fg, xf, w_proj_a[l].astype(BF16), w_proj_b[l].astype(BF16),
                       w_out[l].astype(BF16), expand2, norm_ffn_g[l], seq)
        act = _ffn_up(h, w_up[l].astype(BF16), conv_w[l], conv_b[l], seq)
        g_next = norm_mix_g[l + 1] if l + 1 < depth else jnp.ones((d,), F32)
        xf, h = _ffn_down(act, w_down[l].astype(BF16), xf, g_next)
    return xf.reshape(batch, seq, d)
```

```python
import functools

import numpy as np
import jax
import jax.numpy as jnp
from jax import lax
from jax.experimental import pallas as pl
from jax.experimental.pallas import tpu as pltpu

F32 = jnp.float32
BF16 = jnp.bfloat16
I32 = jnp.int32

D_MODEL = 1024
HEAD_DIM = 64
ROPE_THETA = 10000.0
EPS = 1e-6
NEG = -1e30
TINY = 1e-30
FORCE_SCORE = 1e9

DIL_PAIRS = ((128, 1), (512, 4), (2048, 16))
A_HEADS_PER_GROUP = 8
A_HEADS = 24
A_OUT = 512
A_BAND = 128

B_Q_HEADS = 8
B_KV_HEADS = 2
B_GQA = 4
CMP_LEN = 32
CMP_STRIDE = 16
CMP_HIDDEN = 256
SEL_LEN = 64
SEL_SHIFT = 6
N_SELECT = 16
WIN = 512
B_OUT = 512

D_FF = 2816
CONV_W = 3

A_QKV = 3 * A_HEADS * HEAD_DIM
B_QD = B_Q_HEADS * HEAD_DIM
B_KV = 3 * 2 * B_KV_HEADS * HEAD_DIM
B_GATE = 3 * B_Q_HEADS

LOG2E = 1.4426950408889634
Q_SCALE = LOG2E * HEAD_DIM ** -0.5

LANES = 128
TQ = 128
BAND_BLOCKS = 4
SEL_TK = 512
CH = 32
KEY_CH = 64
VMEM_LIMIT = 48 * 1024 * 1024

_NT = (((1,), (1,)), ((), ()))


def _dot(a, b):
    return jnp.dot(a, b, preferred_element_type=F32)


def _dot_nt(a, b):
    return lax.dot_general(a, b, _NT, preferred_element_type=F32)


def _split(x):
    hi = x.astype(BF16)
    lo = (x - hi.astype(F32)).astype(BF16)
    return jnp.concatenate([hi, lo], axis=1)


def _split_dot(x, w2):
    return _dot(_split(x), w2)


def _params(*sem):
    return pltpu.CompilerParams(dimension_semantics=sem, vmem_limit_bytes=VMEM_LIMIT)


def _rmsnorm_kernel(x_ref, g_ref, h_ref):
    x = x_ref[...]
    y = x * lax.rsqrt(jnp.mean(x * x, axis=-1, keepdims=True) + EPS)
    h_ref[...] = (y * g_ref[...]).astype(h_ref.dtype)


def _rmsnorm(x, g, tm=512):
    t, d = x.shape
    return pl.pallas_call(
        _rmsnorm_kernel,
        grid=(t // tm,),
        in_specs=[pl.BlockSpec((tm, d), lambda i: (i, 0)), pl.BlockSpec((1, d), lambda i: (0, 0))],
        out_specs=pl.BlockSpec((tm, d), lambda i: (i, 0)),
        out_shape=jax.ShapeDtypeStruct((t, d), BF16),
        compiler_params=_params("parallel"),
        name="rmsnorm",
    )(x, g.reshape(1, d))


def _head_norm_rope(y, gain, cos_t, sin_t, seg2):
    ss = _split_dot(y * y, seg2)
    yn = y * lax.rsqrt(ss * (1.0 / HEAD_DIM) + EPS) * gain
    lane = lax.broadcasted_iota(I32, yn.shape, 1)
    partner = jnp.where((lane & 32) == 0, pltpu.roll(yn, 96, 1), pltpu.roll(yn, 32, 1))
    return yn * cos_t + partner * sin_t


def _proj_kernel(h_ref, w_ref, o_ref):
    o_ref[...] = _dot(h_ref[...], w_ref[...]).astype(o_ref.dtype)


def _proj_rope_kernel(h_ref, w_ref, gain_ref, cos_ref, sin_ref, seg_ref, o_ref):
    y = _dot(h_ref[...], w_ref[...])
    cos_t, sin_t, seg2 = cos_ref[...], sin_ref[...], seg_ref[...]
    for c in range(y.shape[1] // LANES):
        sl = slice(c * LANES, (c + 1) * LANES)
        o_ref[:, sl] = _head_norm_rope(y[:, sl], gain_ref[:, sl], cos_t, sin_t, seg2).astype(o_ref.dtype)


def _proj(h, w, out_dtype, tn, tm=512, rope=None, seq=None):
    t, k = h.shape
    n = w.shape[1]
    grid = (n // tn, t // tm)
    h_spec = pl.BlockSpec((tm, k), lambda j, i: (i, 0))
    w_spec = pl.BlockSpec((k, tn), lambda j, i: (0, j))
    o_spec = pl.BlockSpec((tm, tn), lambda j, i: (i, j))
    out_shape = jax.ShapeDtypeStruct((t, n), out_dtype)
    if rope is None:
        return pl.pallas_call(_proj_kernel, grid=grid, in_specs=[h_spec, w_spec], out_specs=o_spec,
                              out_shape=out_shape, compiler_params=_params("parallel", "parallel"),
                              name="proj")(h, w)
    gain, cos_t, sin_t, seg2 = rope
    nseq = seq // tm
    return pl.pallas_call(
        _proj_rope_kernel, grid=grid,
        in_specs=[h_spec, w_spec,
                  pl.BlockSpec((1, tn), lambda j, i: (0, j)),
                  pl.BlockSpec((tm, LANES), lambda j, i: (i % nseq, 0)),
                  pl.BlockSpec((tm, LANES), lambda j, i: (i % nseq, 0)),
                  pl.BlockSpec((2 * LANES, LANES), lambda j, i: (0, 0))],
        out_specs=o_spec, out_shape=out_shape,
        compiler_params=_params("parallel", "parallel"), name="proj_rope",
    )(h, w, gain, cos_t, sin_t, seg2)


def _proj_group_kernel(h_ref, w_ref, gain_ref, cos_ref, sin_ref, seg_ref, o_ref, y_ref, *, n_rope):
    dil, n_sub, _ = o_ref.shape
    y = _dot(h_ref[...], w_ref[...])
    cos_t, sin_t, seg2 = cos_ref[...], sin_ref[...], seg_ref[...]
    for c in range(y.shape[1] // LANES):
        sl = slice(c * LANES, (c + 1) * LANES)
        val = _head_norm_rope(y[:, sl], gain_ref[:, sl], cos_t, sin_t, seg2) if c < n_rope else y[:, sl]
        if dil == 1:
            o_ref[0, :, sl] = val.astype(o_ref.dtype)
        else:
            y_ref[c] = val
            for r in range(dil):
                o_ref[r, :, sl] = y_ref[c, pl.ds(r, n_sub, stride=dil), :].astype(o_ref.dtype)


def _proj_group(h, w, gain, cos_t, sin_t, seg2, dil, batch, seq, tm=512):
    t, k = h.shape
    n = w.shape[1]
    nseq = seq // tm
    return pl.pallas_call(
        functools.partial(_proj_group_kernel, n_rope=gain.shape[1] // LANES),
        grid=(t // tm,),
        in_specs=[pl.BlockSpec((tm, k), lambda i: (i, 0)),
                  pl.BlockSpec((k, n), lambda i: (0, 0)),
                  pl.BlockSpec(gain.shape, lambda i: (0, 0)),
                  pl.BlockSpec((tm, LANES), lambda i: (i % nseq, 0)),
                  pl.BlockSpec((tm, LANES), lambda i: (i % nseq, 0)),
                  pl.BlockSpec((2 * LANES, LANES), lambda i: (0, 0))],
        out_specs=pl.BlockSpec((None, dil, tm // dil, n), lambda i: (i // nseq, 0, i % nseq, 0)),
        out_shape=jax.ShapeDtypeStruct((batch, dil, seq // dil, n), BF16),
        scratch_shapes=[pltpu.VMEM((n // LANES, tm, LANES), F32)],
        compiler_params=_params("parallel"),
        name=f"proj_group_d{dil}",
    )(h, w, gain, cos_t, sin_t, seg2)


def _band_kernel(q_ref, kp_ref, kc_ref, vp_ref, vc_ref, o_ref, lse_ref,
                 kk_ref, vv_ref, s_ref, p_ref, st_ref, bias_ref):
    i = pl.program_id(1)
    n_blk = q_ref.shape[0] // TQ
    n_pair = A_HEADS_PER_GROUP // 2
    kk_ref[0:TQ] = kp_ref[...]
    kk_ref[TQ:] = kc_ref[...]
    vv_ref[0:TQ] = vp_ref[...]
    vv_ref[TQ:] = vc_ref[...]
    row = lax.broadcasted_iota(I32, (TQ, 2 * TQ), 0)
    col = lax.broadcasted_iota(I32, (TQ, 2 * TQ), 1)
    dist = TQ + row - col
    band = (dist >= 0) & (dist <= A_BAND)
    bias_ref[0] = jnp.where(band & ((col >= TQ) | (i > 0)), 0.0, NEG)
    bias_ref[1] = jnp.where(band, 0.0, NEG)
    lane = lax.broadcasted_iota(I32, (TQ, LANES), 1)
    low = lane < HEAD_DIM

    def lanes(c):
        return slice(c * LANES, (c + 1) * LANES)

    def scores(j, p, slot):
        q2 = q_ref[j * TQ:(j + 1) * TQ, lanes(p)].astype(F32)
        qs = jnp.concatenate([jnp.where(low, q2, 0.0), jnp.where(low, 0.0, q2)], axis=0).astype(BF16)
        s_ref[slot] = _dot_nt(qs, kk_ref[j * TQ:(j + 2) * TQ, lanes(p)])

    def softmax(j, slot):
        for r0 in range(0, 2 * TQ, CH):
            rs, ts = slice(r0, r0 + CH), slice(r0 % TQ, r0 % TQ + CH)
            cols = [s_ref[slot, rs, lanes(c)] + bias_ref[min(j, 1), ts, lanes(c)] for c in range(2)]
            m = jnp.max(jnp.maximum(cols[0], cols[1]), axis=-1, keepdims=True)
            ps = [jnp.exp2(x - m) for x in cols]
            for c, pc in enumerate(ps):
                p_ref[slot, rs, lanes(c)] = pc.astype(BF16)
            st_ref[slot, 0, rs, :] = jnp.broadcast_to(m, (CH, LANES))
            st_ref[slot, 1, rs, :] = jnp.broadcast_to(jnp.sum(ps[0] + ps[1], axis=-1, keepdims=True), (CH, LANES))

    def values(j, p, slot, lse_tile):
        l = st_ref[slot, 1]
        o2 = _dot(p_ref[slot], vv_ref[j * TQ:(j + 2) * TQ, lanes(p)]) / l
        o_ref[j * TQ:(j + 1) * TQ, lanes(p)] = jnp.where(low, o2[:TQ], o2[TQ:])
        lse2 = st_ref[slot, 0] + jnp.log(l) * LOG2E
        return jnp.where(lane == 2 * p, lse2[:TQ], jnp.where(lane == 2 * p + 1, lse2[TQ:], lse_tile))

    units = [(j, p) for j in range(n_blk) for p in range(n_pair)]
    scores(*units[0], 0)
    lse_tile = jnp.zeros((TQ, LANES), F32)
    for u, (j, p) in enumerate(units):
        if u + 1 < len(units):
            scores(*units[u + 1], (u + 1) % 2)
        softmax(j, u % 2)
        lse_tile = values(j, p, u % 2, lse_tile)
        if p == n_pair - 1:
            lse_ref[j * TQ:(j + 1) * TQ, :] = lse_tile


def _band_attention(qkv, dil, batch, seq):
    sub = seq // dil
    nb = sub // TQ
    n_blk = min(BAND_BLOCKS, nb)
    rows = n_blk * TQ
    steps = nb // n_blk
    n_sub = batch * dil
    flat = qkv.reshape(n_sub * sub, qkv.shape[-1])

    def cur(n, i):
        return n * steps + i

    def prev(n, i):
        return n * nb + jnp.maximum(i * n_blk - 1, 0)

    o, lse = pl.pallas_call(
        _band_kernel,
        grid=(n_sub, steps),
        in_specs=[
            pl.BlockSpec((rows, A_OUT), lambda n, i: (cur(n, i), 0)),
            pl.BlockSpec((TQ, A_OUT), lambda n, i: (prev(n, i), 1)),
            pl.BlockSpec((rows, A_OUT), lambda n, i: (cur(n, i), 1)),
            pl.BlockSpec((TQ, A_OUT), lambda n, i: (prev(n, i), 2)),
            pl.BlockSpec((rows, A_OUT), lambda n, i: (cur(n, i), 2)),
        ],
        out_specs=[pl.BlockSpec((rows, A_OUT), lambda n, i: (cur(n, i), 0)),
                   pl.BlockSpec((rows, LANES), lambda n, i: (cur(n, i), 0))],
        out_shape=[jax.ShapeDtypeStruct((n_sub * sub, A_OUT), F32),
                   jax.ShapeDtypeStruct((n_sub * sub, LANES), F32)],
        scratch_shapes=[pltpu.VMEM((rows + TQ, A_OUT), BF16), pltpu.VMEM((rows + TQ, A_OUT), BF16),
                        pltpu.VMEM((2, 2 * TQ, 2 * TQ), F32), pltpu.VMEM((2, 2 * TQ, 2 * TQ), BF16),
                        pltpu.VMEM((2, 2, 2 * TQ, LANES), F32), pltpu.VMEM((2, TQ, 2 * TQ), F32)],
        compiler_params=_params("parallel", "parallel"),
        name=f"band_attention_d{dil}",
    )(flat, flat, flat, flat, flat)
    return o.reshape(batch, dil, sub, A_OUT), lse.reshape(batch, dil, sub, LANES)


def _compress_kernel(x_ref, pe_ref, w1_ref, w2_ref, gain_ref, cos_ref, sin_ref, seg_ref, o_ref):
    kv = pl.program_id(0)
    n = o_ref.shape[0]
    top = jnp.zeros((n, w1_ref.shape[-1]), F32)
    bot = jnp.zeros((n, w1_ref.shape[-1]), F32)
    for p in range(CMP_STRIDE):
        xp = x_ref[pl.ds(p, n, stride=CMP_STRIDE), :]
        top = top + _dot((xp + pe_ref[p:p + 1, :]).astype(BF16), w1_ref[p])
        bot = bot + _dot((xp + pe_ref[CMP_STRIDE + p:CMP_STRIDE + p + 1, :]).astype(BF16), w1_ref[CMP_STRIDE + p])
    hid = jax.nn.gelu(top + pltpu.roll(bot, n - 1, 0))
    out = _dot(hid.astype(BF16), w2_ref[...])

    @pl.when(kv == 0)
    def _():
        o_ref[...] = _head_norm_rope(out, gain_ref[...], cos_ref[...], sin_ref[...], seg_ref[...]).astype(o_ref.dtype)

    @pl.when(kv != 0)
    def _():
        o_ref[...] = out.astype(o_ref.dtype)


def _compress(fg, raw_block, pe, w1e, w2e, gain, cos_c, sin_c, seg2, batch, seq):
    n_chunk = seq // CMP_STRIDE
    return pl.pallas_call(
        _compress_kernel,
        grid=(2, batch),
        in_specs=[
            pl.BlockSpec((seq, LANES), lambda kv, b: (b, raw_block + kv)),
            pl.BlockSpec((None,) + pe.shape[1:], lambda kv, b: (kv, 0, 0)),
            pl.BlockSpec((None,) + w1e.shape[1:], lambda kv, b: (kv, 0, 0, 0)),
            pl.BlockSpec((None,) + w2e.shape[1:], lambda kv, b: (kv, 0, 0)),
            pl.BlockSpec((1, LANES), lambda kv, b: (0, 0)),
            pl.BlockSpec((n_chunk, LANES), lambda kv, b: (0, 0)),
            pl.BlockSpec((n_chunk, LANES), lambda kv, b: (0, 0)),
            pl.BlockSpec((2 * LANES, LANES), lambda kv, b: (0, 0)),
        ],
        out_specs=pl.BlockSpec((None, n_chunk, LANES), lambda kv, b: (kv, b, 0)),
        out_shape=jax.ShapeDtypeStruct((2, batch * n_chunk, LANES), BF16),
        compiler_params=_params("parallel", "parallel"),
        name="compress",
    )(fg, pe, w1e, w2e, gain, cos_c, sin_c, seg2)


def _nsa_kernel(q_ref, ke_ref, vselt_ref, kwin_ref, vwin_ref, kc_ref, vc_ref, gate_ref, wslct_ref, o_ref,
                qa0, qa1, qat0, qat1, s0, s1, p0, p1, st0, st1, stt0, stt1, acc0, acc1, acct0, acct1, ot0, ot1,
                bias_ref, biast_ref, cmask_ref, imp_ref, *, n_cmp):
    qa, qat, s_scr, p_scr = (qa0, qa1), (qat0, qat1), (s0, s1), (p0, p1)
    st, stt, acc, acct, ot = (st0, st1), (stt0, stt1), (acc0, acc1), (acct0, acct1), (ot0, ot1)
    m_row, l_row, a_row = 0, 1, 2
    qi = pl.program_id(1)
    q0 = qi * TQ
    rows = B_GQA * TQ
    n_ck = kc_ref.shape[0]
    span = WIN + TQ
    n_full = q0 // SEL_TK
    w0 = pl.multiple_of(jnp.maximum(q0 - WIN, 0), TQ)
    lane = lax.broadcasted_iota(I32, (TQ, LANES), 1)
    low = lane < HEAD_DIM
    gates = jax.nn.sigmoid(gate_ref[...])

    def token_key(width):
        return (q0 + lax.broadcasted_iota(I32, (TQ, width), 0), lax.broadcasted_iota(I32, (TQ, width), 1))

    tr, cc = token_key(n_ck)
    cmp_valid = (cc * CMP_STRIDE + (CMP_LEN - 1) <= tr) & (cc < n_cmp)
    bias_ref[0, :, :n_ck] = jnp.where(cmp_valid, 0.0, NEG)
    cmask_ref[...] = jnp.where(cmp_valid, 1.0, 0.0)
    tr, cc = token_key(span)
    dist = tr - (w0 + cc)
    bias_ref[1, :, :span] = jnp.where((dist >= 0) & (dist < WIN), 0.0, NEG)
    key_pos = n_full * SEL_TK + lax.broadcasted_iota(I32, (SEL_TK, TQ), 0)
    biast_ref[...] = jnp.where(key_pos <= q0 + lax.broadcasted_iota(I32, (SEL_TK, TQ), 1), 0.0, NEG)

    blk = lax.broadcasted_iota(I32, (LANES, TQ), 0)
    tok = q0 + lax.broadcasted_iota(I32, (LANES, TQ), 1)
    cur = jnp.right_shift(tok, SEL_SHIFT)
    forced = (blk == 0) | (blk == cur) | (blk == cur - 1)
    causal_blk = blk * SEL_LEN <= tok
    blk_f = blk.astype(F32)

    def gate_col(branch, h):
        base = branch * B_Q_HEADS + h * B_GQA
        return jnp.concatenate([gates[:, base + g:base + g + 1] for g in range(B_GQA)], axis=0)

    def lanes(c):
        return slice(c * LANES, (c + 1) * LANES)

    def score_cols(h, r0, width, mask):
        ts = r0 % TQ
        cols = [s_scr[h][r0:r0 + CH, lanes(c)] for c in range(width // LANES)]
        if mask is not None:
            cols = [x + bias_ref[mask, ts:ts + CH, lanes(c)] for c, x in enumerate(cols)]
        return cols

    def row_max(cols):
        mx = cols[0]
        for x in cols[1:]:
            mx = jnp.maximum(mx, x)
        return jnp.max(mx, axis=-1, keepdims=True)

    def row_sum(cols):
        tot = cols[0]
        for x in cols[1:]:
            tot = tot + x
        return jnp.sum(tot, axis=-1, keepdims=True)

    def softmax_cmp(h):
        for r0 in range(0, rows, CH):
            ts = r0 % TQ
            cols = score_cols(h, r0, n_ck, 0)
            m = row_max(cols)
            es = [jnp.exp2(x - m) * cmask_ref[ts:ts + CH, lanes(c)] for c, x in enumerate(cols)]
            rinv = 1.0 / jnp.maximum(row_sum(es), TINY)
            for c, e in enumerate(es):
                pn = e * rinv
                p_scr[h][r0:r0 + CH, lanes(c)] = pn.astype(BF16)
                if r0 < TQ:
                    imp_ref[h, ts:ts + CH, lanes(c)] = pn
                else:
                    imp_ref[h, ts:ts + CH, lanes(c)] += pn

    def softmax_online(h, masked):
        for g in range(B_GQA):
            ls = lanes(g)

            def piece(k):
                x = s_scr[h][k:k + KEY_CH, ls]
                return x + biast_ref[k:k + KEY_CH, :] if masked else x

            mx = piece(0)
            for k in range(KEY_CH, SEL_TK, KEY_CH):
                mx = jnp.maximum(mx, piece(k))
            m_old = stt[h][m_row, :, ls]
            m_new = jnp.maximum(m_old, jnp.max(mx, axis=0, keepdims=True))
            alpha = jnp.exp2(m_old - m_new)
            m_1 = m_new[0:1, :]
            tot = None
            for k in range(0, SEL_TK, KEY_CH):
                pc = jnp.exp2(piece(k) - m_1)
                tot = pc if tot is None else tot + pc
                p_scr[h][k:k + KEY_CH, ls] = pc.astype(BF16)
            stt[h][l_row, :, ls] = alpha * stt[h][l_row, :, ls] + jnp.sum(tot, axis=0, keepdims=True)
            stt[h][m_row, :, ls] = m_new
            stt[h][a_row, :, ls] = alpha

    def softmax_once(h, width, mask):
        for r0 in range(0, rows, CH):
            rs = slice(r0, r0 + CH)
            cols = score_cols(h, r0, width, mask)
            m = row_max(cols)
            ps = [jnp.exp2(x - m) for x in cols]
            for c, pc in enumerate(ps):
                p_scr[h][rs, lanes(c)] = pc.astype(BF16)
            st[h][l_row, rs, :] = jnp.broadcast_to(row_sum(ps), (CH, LANES))

    def score_tile(h, kt):
        k0 = pl.multiple_of(kt * SEL_TK, SEL_TK)
        s_scr[h][:, :SEL_TK] = _dot(ke_ref[pl.ds(k0, SEL_TK), :], qat[h][...])

    def value_tile(h, kt, live=None):
        pv = _dot(vselt_ref[kt], p_scr[h][:, :SEL_TK])
        if live is not None:
            pv = jnp.where(live, pv, 0.0)
        acct[h][...] = stt[h][a_row, 0:1, :] * acct[h][...] + pv

    for h in range(B_KV_HEADS):
        for g in range(B_GQA):
            c, half = (h * B_GQA + g) // 2, g % 2
            chunk = q_ref[:, lanes(c)].astype(F32)
            if half != h:
                chunk = pltpu.roll(chunk, HEAD_DIM, 1)
            chunk = jnp.where(low if h == 0 else jnp.logical_not(low), chunk, 0.0)
            qa[h][g * TQ:(g + 1) * TQ, :] = chunk.astype(BF16)
            qat[h][:LANES, lanes(g)] = chunk.T.astype(BF16)
        s_scr[h][:, :n_ck] = _dot_nt(qa[h][...], kc_ref[...])

    for h in range(B_KV_HEADS):
        softmax_cmp(h)
        ot[h][...] = gate_col(0, h) * _dot(p_scr[h][:, :n_ck], vc_ref[...])

        p_slc = _dot_nt(wslct_ref[...], _split(imp_ref[h]))
        score = jnp.where(forced, FORCE_SCORE, jnp.where(causal_blk, p_slc, -1.0))
        chosen = jnp.zeros((LANES, TQ), F32)
        for _ in range(N_SELECT):
            best = jnp.max(score, axis=0, keepdims=True)
            first = jnp.min(jnp.where(score == best, blk_f, float(LANES)), axis=0, keepdims=True)
            pick = blk_f == first
            chosen = jnp.where(pick, 1.0, chosen)
            score = jnp.where(pick, -2.0, score)
        bias = jnp.where(chosen > 0.0, 0.0, NEG).astype(BF16)
        for g in range(B_GQA):
            qat[h][LANES:, lanes(g)] = bias

        stt[h][m_row] = jnp.full((8, rows), NEG, F32)
        stt[h][l_row] = jnp.zeros((8, rows), F32)
        stt[h][a_row] = jnp.ones((8, rows), F32)
        acct[h][...] = jnp.zeros((LANES, rows), F32)

    score_tile(0, 0)

    def sweep(kt, carry):
        value_tile(1, jnp.maximum(kt - 1, 0), live=kt > 0)
        score_tile(1, kt)
        softmax_online(0, False)
        value_tile(0, kt)
        score_tile(0, kt + 1)
        softmax_online(1, False)
        return carry

    lax.fori_loop(0, n_full, sweep, 0)
    value_tile(1, jnp.maximum(n_full - 1, 0), live=n_full > 0)
    score_tile(1, n_full)
    softmax_online(0, True)
    value_tile(0, n_full)
    softmax_online(1, True)
    value_tile(1, n_full)

    kw = kwin_ref[pl.ds(w0, span), :]
    vw = vwin_ref[pl.ds(w0, span), :]
    for h in range(B_KV_HEADS):
        acc[h][...] = (acct[h][...] / stt[h][l_row, 0:1, :]).T
        s_scr[h][:, :span] = _dot_nt(qa[h][...], kw)
    for h in range(B_KV_HEADS):
        softmax_once(h, span, 1)
        o_win = _dot(p_scr[h][:, :span], vw) / st[h][l_row]
        o_tot = ot[h][...] + gate_col(1, h) * acc[h][...] + gate_col(2, h) * o_win

        for c2 in range(B_GQA // 2):
            halves = []
            for half in range(2):
                g = 2 * c2 + half
                piece = o_tot[g * TQ:(g + 1) * TQ]
                halves.append(piece if half == h else pltpu.roll(piece, HEAD_DIM, 1))
            o_ref[:, lanes(h * (B_GQA // 2) + c2)] = jnp.where(low, halves[0], halves[1]).astype(o_ref.dtype)


def _nsa(bqk, ke, vselt, bv, cmp_kv, fgate, wslct2, batch, seq, gate_block):
    t = bqk.shape[0]
    nq = seq // TQ
    n_ck = cmp_kv.shape[1] // batch
    n_cmp = seq // CMP_STRIDE - CMP_LEN // CMP_STRIDE + 1
    rows = B_GQA * TQ
    width = max(WIN + TQ, SEL_TK, n_ck)
    return pl.pallas_call(
        functools.partial(_nsa_kernel, n_cmp=n_cmp),
        grid=(batch, nq),
        in_specs=[
            pl.BlockSpec((TQ, B_QD), lambda b, i: (b * nq + i, 0)),
            pl.BlockSpec((seq, 2 * LANES), lambda b, i: (b, 0)),
            pl.BlockSpec((None,) + vselt.shape[1:], lambda b, i: (b, 0, 0, 0)),
            pl.BlockSpec((seq, LANES), lambda b, i: (b, B_QD // LANES + 1)),
            pl.BlockSpec((seq, LANES), lambda b, i: (b, 1)),
            pl.BlockSpec((None, n_ck, LANES), lambda b, i: (0, b, 0)),
            pl.BlockSpec((None, n_ck, LANES), lambda b, i: (1, b, 0)),
            pl.BlockSpec((TQ, LANES), lambda b, i: (b * nq + i, gate_block)),
            pl.BlockSpec(wslct2.shape, lambda b, i: (0, 0)),
        ],
        out_specs=pl.BlockSpec((TQ, B_OUT), lambda b, i: (b * nq + i, 0)),
        out_shape=jax.ShapeDtypeStruct((t, B_OUT), BF16),
        scratch_shapes=(
            [pltpu.VMEM((rows, LANES), BF16)] * 2 +
            [pltpu.VMEM((2 * LANES, rows), BF16)] * 2 +
            [pltpu.VMEM((rows, width), F32)] * 2 +
            [pltpu.VMEM((rows, width), BF16)] * 2 +
            [pltpu.VMEM((3, rows, LANES), F32)] * 2 +
            [pltpu.VMEM((3, 8, rows), F32)] * 2 +
            [pltpu.VMEM((rows, LANES), F32)] * 2 +
            [pltpu.VMEM((LANES, rows), F32)] * 2 +
            [pltpu.VMEM((rows, LANES), F32)] * 2 +
            [pltpu.VMEM((2, TQ, width), F32), pltpu.VMEM((SEL_TK, TQ), F32),
             pltpu.VMEM((TQ, n_ck), F32), pltpu.VMEM((2, TQ, n_ck), F32)]),
        compiler_params=_params("parallel", "parallel"),
        name="nsa_mixer",
    )(bqk, ke, vselt, bqk, bv, cmp_kv, cmp_kv, fgate, wslct2)


def _merge_kernel(o0_ref, o1_ref, o2_ref, l0_ref, l1_ref, l2_ref, ob_ref, m0_ref, m1_ref, x_ref,
                  wpa_ref, wpb_ref, wout_ref, exp_ref, g_ref, xo_ref, h_ref, osc_ref, lsc_ref):
    def token_order(ref, scr, slot):
        d, n, width = ref.shape
        if d == 1:
            return ref[0]
        chunks = width // LANES
        for c in range(chunks):
            for r in range(d):
                scr[slot * chunks + c, pl.ds(r, n, stride=d), :] = ref[r, :, c * LANES:(c + 1) * LANES]
        return jnp.concatenate([scr[slot * chunks + c] for c in range(chunks)], axis=1)

    l0, l1, l2 = token_order(l0_ref, lsc_ref, 0), token_order(l1_ref, lsc_ref, 0), token_order(l2_ref, lsc_ref, 1)
    o0, o1, o2 = token_order(o0_ref, osc_ref, 0), token_order(o1_ref, osc_ref, 0), token_order(o2_ref, osc_ref, 1)
    mx = jnp.maximum(jnp.maximum(l0, l1), l2)
    e0, e1, e2 = jnp.exp2(l0 - mx), jnp.exp2(l1 - mx), jnp.exp2(l2 - mx)
    den = e0 + e1 + e2
    ex = exp_ref[...]
    out_a = _split_dot(e0 / den, ex) * o0 + _split_dot(e1 / den, ex) * o1 + _split_dot(e2 / den, ex) * o2
    pa = _dot(out_a.astype(BF16), wpa_ref[...])
    pb = _dot(ob_ref[...], wpb_ref[...])
    mixed = jax.nn.sigmoid(m0_ref[...]) * pa + jax.nn.sigmoid(m1_ref[...]) * pb
    x = x_ref[...] + _dot(mixed.astype(BF16), wout_ref[...])
    xo_ref[...] = x
    y = x * lax.rsqrt(jnp.mean(x * x, axis=-1, keepdims=True) + EPS)
    h_ref[...] = (y * g_ref[...]).astype(h_ref.dtype)


def _merge(o_groups, lse_groups, out_b, fgate, x, wpa, wpb, wout, expand2, g_next, seq, tm=256):
    t, d = x.shape
    nseq = seq // tm
    row = lambda w: pl.BlockSpec((tm, w), lambda i: (i, 0))
    full = lambda a: pl.BlockSpec(a.shape, lambda i: (0,) * a.ndim)
    grouped = lambda a: pl.BlockSpec((None, a.shape[1], tm // a.shape[1], a.shape[3]),
                                     lambda i: (i // nseq, 0, i % nseq, 0))
    g_next = g_next.reshape(1, d)
    return pl.pallas_call(
        _merge_kernel,
        grid=(t // tm,),
        in_specs=[grouped(a) for a in o_groups] + [grouped(a) for a in lse_groups] + [
            row(B_OUT), pl.BlockSpec((tm, d), lambda i: (i, 0)), pl.BlockSpec((tm, d), lambda i: (i, 1)), row(d),
            full(wpa), full(wpb), full(wout), full(expand2), full(g_next)],
        out_specs=[row(d), row(d)],
        out_shape=[jax.ShapeDtypeStruct((t, d), F32), jax.ShapeDtypeStruct((t, d), BF16)],
        scratch_shapes=[pltpu.VMEM((2 * A_OUT // LANES, tm, LANES), F32), pltpu.VMEM((2, tm, LANES), F32)],
        compiler_params=_params("parallel"),
        name="merge_out",
    )(*o_groups, *lse_groups, out_b, fgate, fgate, x, wpa, wpb, wout, expand2, g_next)


def _ffn_up_kernel(h_ref, wg_ref, wu_ref, cw_ref, cb_ref, o_ref, gbuf_ref, *, tiles_per_seq):
    i = pl.program_id(1)
    tm = h_ref.shape[0]
    h = h_ref[...]

    @pl.when(i % tiles_per_seq == 0)
    def _():
        gbuf_ref[0:8, :] = jnp.zeros((8, gbuf_ref.shape[1]), F32)

    @pl.when(i % tiles_per_seq != 0)
    def _():
        gbuf_ref[0:8, :] = gbuf_ref[tm:tm + 8, :]

    gbuf_ref[8:tm + 8, :] = _dot(h, wg_ref[...])
    u = _dot(h, wu_ref[...])
    conv = cb_ref[...] + cw_ref[0:1, :] * gbuf_ref[pl.ds(6, tm), :]
    conv = conv + cw_ref[1:2, :] * gbuf_ref[pl.ds(7, tm), :]
    conv = conv + cw_ref[2:3, :] * gbuf_ref[pl.ds(8, tm), :]
    o_ref[...] = (jax.nn.silu(conv) * u).astype(o_ref.dtype)


def _ffn_up(h, w_up, conv_w, conv_b, seq, tm=512, tn=1408):
    t, d = h.shape
    nj = D_FF // tn
    return pl.pallas_call(
        functools.partial(_ffn_up_kernel, tiles_per_seq=seq // tm),
        grid=(nj, t // tm),
        in_specs=[
            pl.BlockSpec((tm, d), lambda j, i: (i, 0)),
            pl.BlockSpec((d, tn), lambda j, i: (0, j)),
            pl.BlockSpec((d, tn), lambda j, i: (0, nj + j)),
            pl.BlockSpec((CONV_W, tn), lambda j, i: (0, j)),
            pl.BlockSpec((1, tn), lambda j, i: (0, j)),
        ],
        out_specs=pl.BlockSpec((tm, tn), lambda j, i: (i, j)),
        out_shape=jax.ShapeDtypeStruct((t, D_FF), BF16),
        scratch_shapes=[pltpu.VMEM((tm + 8, tn), F32)],
        compiler_params=_params("arbitrary", "arbitrary"),
        name="ffn_up_conv",
    )(h, w_up, w_up, conv_w, conv_b.reshape(1, D_FF))


def _ffn_down_kernel(a_ref, w_ref, x_ref, g_ref, xo_ref, h_ref):
    x = x_ref[...] + _dot(a_ref[...], w_ref[...])
    xo_ref[...] = x
    y = x * lax.rsqrt(jnp.mean(x * x, axis=-1, keepdims=True) + EPS)
    h_ref[...] = (y * g_ref[...]).astype(h_ref.dtype)


def _ffn_down(act, w_down, x, g_next, tm=512):
    t, d = x.shape
    row = lambda w: pl.BlockSpec((tm, w), lambda i: (i, 0))
    return pl.pallas_call(
        _ffn_down_kernel,
        grid=(t // tm,),
        in_specs=[row(D_FF), pl.BlockSpec(w_down.shape, lambda i: (0, 0)), row(d),
                  pl.BlockSpec((1, d), lambda i: (0, 0))],
        out_specs=[row(d), row(d)],
        out_shape=[jax.ShapeDtypeStruct((t, d), F32), jax.ShapeDtypeStruct((t, d), BF16)],
        compiler_params=_params("parallel"),
        name="ffn_down",
    )(act, w_down, x, g_next.reshape(1, d))


def _rope_tables(pos):
    half = HEAD_DIM // 2
    inv_freq = ROPE_THETA ** (-jnp.arange(half, dtype=F32) / half)
    ang = pos.astype(F32)[:, None] * inv_freq[None, :]
    c, s = jnp.cos(ang), jnp.sin(ang)
    return jnp.tile(c, (1, 4)), jnp.tile(jnp.concatenate([-s, s], axis=1), (1, 2))


def _constants(n_ck, seq):
    lane = np.arange(LANES)
    seg = (lane[:, None] // HEAD_DIM == lane[None, :] // HEAD_DIM).astype(np.float32)
    seg2 = np.concatenate([seg, seg], axis=0)
    ratio = SEL_LEN // CMP_STRIDE
    wslc = np.zeros((n_ck, LANES), np.float32)
    for off in range(CMP_LEN // CMP_STRIDE):
        for i in range(n_ck):
            if i + off < n_ck:
                wslc[i, (i + off) // ratio] += 1.0
    wslct2 = np.concatenate([wslc, wslc], axis=0).T
    expand = np.zeros((LANES, A_OUT), np.float32)
    for j in range(A_HEADS_PER_GROUP):
        expand[j, j * HEAD_DIM:(j + 1) * HEAD_DIM] = 1.0
    expand2 = np.concatenate([expand, expand], axis=0)
    onehot = (np.arange(seq)[:, None] // SEL_LEN == lane[None, :]).astype(np.float32)
    return (jnp.asarray(seg2, BF16), jnp.asarray(wslct2, BF16), jnp.asarray(expand2, BF16),
            jnp.asarray(onehot, BF16))


def kernel(x, norm_mix_g, w_in, a_q_g, a_k_g, b_q_g, b_k_g, cmp_pos, cmp_w1, cmp_w2, w_proj_a, w_proj_b,
           w_out, norm_ffn_g, w_up, conv_w, conv_b, w_down):
    batch, seq, d = x.shape
    depth = w_in.shape[0]
    t = batch * seq
    assert d == D_MODEL and seq % (16 * TQ) == 0 and seq // SEL_LEN <= LANES
    n_ck = seq // CMP_STRIDE
    seg2, wslct2, expand2, onehot = _constants(n_ck, seq)
    onehot = jnp.tile(onehot, (batch, 1))
    cos_t, sin_t = _rope_tables(jnp.arange(seq))
    cos_c, sin_c = _rope_tables(jnp.arange(n_ck) * CMP_STRIDE + CMP_LEN - 1)
    eye_h = jnp.eye(B_KV_HEADS, dtype=F32)
    o_kq, o_kv, o_gate, o_merge = A_QKV, A_QKV + B_QD, A_QKV + B_QD + B_KV, A_QKV + B_QD + B_KV + B_GATE
    n_a = A_HEADS * HEAD_DIM
    f_gate_block = 2 * D_MODEL // LANES

    xf = x.reshape(t, d)
    h = _rmsnorm(xf, norm_mix_g[0])
    for l in range(depth):
        w = w_in[l]
        kv = lambda i: w[:, o_kv + i * LANES:o_kv + (i + 1) * LANES]
        w_bqk = jnp.concatenate([w[:, o_kq:o_kv], kv(2), kv(4)], axis=1).astype(BF16)
        w_bv = jnp.concatenate([kv(3), kv(5)], axis=1).astype(BF16)
        w_f = jnp.concatenate([w[:, o_merge:], w[:, o_gate:o_merge],
                               jnp.zeros((d, LANES - B_GATE), F32), kv(0), kv(1)], axis=1).astype(BF16)
        gain_a = jnp.concatenate([jnp.tile(a_q_g[l] * Q_SCALE, A_HEADS_PER_GROUP),
                                  jnp.tile(a_k_g[l], A_HEADS_PER_GROUP)]).reshape(1, -1)
        gain_b = jnp.concatenate([jnp.tile(b_q_g[l] * Q_SCALE, B_Q_HEADS), jnp.tile(b_k_g[l, 1], B_KV_HEADS),
                                  jnp.tile(b_k_g[l, 2], B_KV_HEADS)]).reshape(1, -1)

        o_groups, lse_groups = [], []
        for gi, (_, dil) in enumerate(DIL_PAIRS):
            cols = [w[:, part * n_a + gi * A_OUT:part * n_a + (gi + 1) * A_OUT] for part in range(3)]
            qkv = _proj_group(h, jnp.concatenate(cols, axis=1).astype(BF16), gain_a, cos_t, sin_t, seg2,
                              dil, batch, seq)
            o_g, lse_g = _band_attention(qkv, dil, batch, seq)
            o_groups.append(o_g)
            lse_groups.append(lse_g)

        bqk = _proj(h, w_bqk, BF16, tn=768, rope=(gain_b, cos_t, sin_t, seg2), seq=seq)
        bv = _proj(h, w_bv, BF16, tn=256)
        fg = _proj(h, w_f, F32, tn=w_f.shape[1], tm=256)
        pe = jnp.tile(cmp_pos[l], (1, 1, B_KV_HEADS))
        w1 = cmp_w1[l].reshape(2, CMP_LEN, HEAD_DIM, CMP_HIDDEN)
        w1e = jnp.einsum('kpdn,hj->kphdjn', w1, eye_h).reshape(2, CMP_LEN, LANES, 2 * CMP_HIDDEN)
        w2e = jnp.einsum('knd,hj->khnjd', cmp_w2[l], eye_h).reshape(2, 2 * CMP_HIDDEN, LANES)
        gain_c = jnp.tile(b_k_g[l, 0], B_KV_HEADS).reshape(1, LANES)
        cmp_kv = _compress(fg, f_gate_block + 1, pe, w1e.astype(BF16), w2e.astype(BF16), gain_c,
                           cos_c, sin_c, seg2, batch, seq)
        ke = jnp.concatenate([bqk[:, B_QD:B_QD + LANES], onehot], axis=1)
        vselt = bv[:, :LANES].reshape(batch, seq // SEL_TK, SEL_TK, LANES).transpose(0, 1, 3, 2)
        out_b = _nsa(bqk, ke, vselt, bv, cmp_kv, fg, wslct2, batch, seq, f_gate_block)

        xf, h = _merge(o_groups, lse_groups, out_b, fg, xf, w_proj_a[l].astype(BF16), w_proj_b[l].astype(BF16),
                       w_out[l].astype(BF16), expand2, norm_ffn_g[l], seq)
        act = _ffn_up(h, w_up[l].astype(BF16), conv_w[l], conv_b[l], seq)
        g_next = norm_mix_g[l + 1] if l + 1 < depth else jnp.ones((d,), F32)
        xf, h = _ffn_down(act, w_down[l].astype(BF16), xf, g_next)
    return xf.reshape(batch, seq, d)
```

```python
import functools

import numpy as np
import jax
import jax.numpy as jnp
from jax import lax
from jax.experimental import pallas as pl
from jax.experimental.pallas import tpu as pltpu

F32 = jnp.float32
BF16 = jnp.bfloat16
I32 = jnp.int32

D_MODEL = 1024
HEAD_DIM = 64
ROPE_THETA = 10000.0
EPS = 1e-6
NEG = -1e30
TINY = 1e-30
FORCE_SCORE = 1e9

DIL_PAIRS = ((128, 1), (512, 4), (2048, 16))
A_HEADS_PER_GROUP = 8
A_HEADS = 24
A_OUT = 512
A_BAND = 128

B_Q_HEADS = 8
B_KV_HEADS = 2
B_GQA = 4
CMP_LEN = 32
CMP_STRIDE = 16
CMP_HIDDEN = 256
SEL_LEN = 64
SEL_SHIFT = 6
N_SELECT = 16
WIN = 512
B_OUT = 512

D_FF = 2816
CONV_W = 3

A_QKV = 3 * A_HEADS * HEAD_DIM
B_QD = B_Q_HEADS * HEAD_DIM
B_KV = 3 * 2 * B_KV_HEADS * HEAD_DIM
B_GATE = 3 * B_Q_HEADS

LOG2E = 1.4426950408889634
Q_SCALE = LOG2E * HEAD_DIM ** -0.5

LANES = 128
MXU_COLS = 256
TQ = 128
BAND_BLOCKS = 4
SEL_TK = 512
CH = 32
VMEM_LIMIT = 48 * 1024 * 1024

_NT = (((1,), (1,)), ((), ()))


def _dot(a, b):
    return jnp.dot(a, b, preferred_element_type=F32)


def _dot_nt(a, b):
    return lax.dot_general(a, b, _NT, preferred_element_type=F32)


def _split(x):
    hi = x.astype(BF16)
    lo = (x - hi.astype(F32)).astype(BF16)
    return jnp.concatenate([hi, lo], axis=1)


def _split_dot(x, w2):
    return _dot(_split(x), w2)


def _params(*sem):
    return pltpu.CompilerParams(dimension_semantics=sem, vmem_limit_bytes=VMEM_LIMIT)


def _rmsnorm_kernel(x_ref, g_ref, h_ref):
    x = x_ref[...]
    y = x * lax.rsqrt(jnp.mean(x * x, axis=-1, keepdims=True) + EPS)
    h_ref[...] = (y * g_ref[...]).astype(h_ref.dtype)


def _rmsnorm(x, g, tm=512):
    t, d = x.shape
    return pl.pallas_call(
        _rmsnorm_kernel,
        grid=(t // tm,),
        in_specs=[pl.BlockSpec((tm, d), lambda i: (i, 0)), pl.BlockSpec((1, d), lambda i: (0, 0))],
        out_specs=pl.BlockSpec((tm, d), lambda i: (i, 0)),
        out_shape=jax.ShapeDtypeStruct((t, d), BF16),
        compiler_params=_params("parallel"),
        name="rmsnorm",
    )(x, g.reshape(1, d))


def _head_norm_rope(y, gain, cos_t, sin_t, seg):
    ms = _dot((y * y).astype(BF16), seg)
    yn = y * lax.rsqrt(ms + EPS) * gain
    lane = lax.broadcasted_iota(I32, yn.shape, 1)
    partner = jnp.where((lane & 32) == 0, pltpu.roll(yn, 96, 1), pltpu.roll(yn, 32, 1))
    return yn * cos_t + partner * sin_t


def _proj_kernel(h_ref, w_ref, o_ref):
    o_ref[...] = _dot(h_ref[...], w_ref[...]).astype(o_ref.dtype)


def _proj_rope_kernel(h_ref, w_ref, gain_ref, cos_ref, sin_ref, seg_ref, o_ref):
    y = _dot(h_ref[...], w_ref[...])
    cos_t, sin_t, seg = cos_ref[...], sin_ref[...], seg_ref[...]
    for c in range(y.shape[1] // LANES):
        sl = slice(c * LANES, (c + 1) * LANES)
        o_ref[:, sl] = _head_norm_rope(y[:, sl], gain_ref[:, sl], cos_t, sin_t, seg).astype(o_ref.dtype)


def _proj(h, w, out_dtype, tn, tm=512, rope=None, seq=None):
    t, k = h.shape
    n = w.shape[1]
    grid = (n // tn, t // tm)
    h_spec = pl.BlockSpec((tm, k), lambda j, i: (i, 0))
    w_spec = pl.BlockSpec((k, tn), lambda j, i: (0, j))
    o_spec = pl.BlockSpec((tm, tn), lambda j, i: (i, j))
    out_shape = jax.ShapeDtypeStruct((t, n), out_dtype)
    if rope is None:
        return pl.pallas_call(_proj_kernel, grid=grid, in_specs=[h_spec, w_spec], out_specs=o_spec,
                              out_shape=out_shape, compiler_params=_params("parallel", "parallel"),
                              name="proj")(h, w)
    gain, cos_t, sin_t, seg = rope
    nseq = seq // tm
    return pl.pallas_call(
        _proj_rope_kernel, grid=grid,
        in_specs=[h_spec, w_spec,
                  pl.BlockSpec((1, tn), lambda j, i: (0, j)),
                  pl.BlockSpec((tm, LANES), lambda j, i: (i % nseq, 0)),
                  pl.BlockSpec((tm, LANES), lambda j, i: (i % nseq, 0)),
                  pl.BlockSpec((LANES, LANES), lambda j, i: (0, 0))],
        out_specs=o_spec, out_shape=out_shape,
        compiler_params=_params("parallel", "parallel"), name="proj_rope",
    )(h, w, gain, cos_t, sin_t, seg)


def _proj_group_kernel(h_ref, w_ref, gain_ref, cos_ref, sin_ref, seg_ref, o_ref, y_ref, *, n_rope):
    dil, n_sub, _ = o_ref.shape
    y = _dot(h_ref[...], w_ref[...])
    cos_t, sin_t, seg = cos_ref[...], sin_ref[...], seg_ref[...]
    for c in range(y.shape[1] // LANES):
        sl = slice(c * LANES, (c + 1) * LANES)
        val = _head_norm_rope(y[:, sl], gain_ref[:, sl], cos_t, sin_t, seg) if c < n_rope else y[:, sl]
        if dil == 1:
            o_ref[0, :, sl] = val.astype(o_ref.dtype)
        else:
            y_ref[c] = val
            for r in range(dil):
                o_ref[r, :, sl] = y_ref[c, pl.ds(r, n_sub, stride=dil), :].astype(o_ref.dtype)


def _proj_group(h, w, gain, cos_t, sin_t, seg, dil, batch, seq, tm=512):
    t, k = h.shape
    n = w.shape[1]
    nseq = seq // tm
    return pl.pallas_call(
        functools.partial(_proj_group_kernel, n_rope=gain.shape[1] // LANES),
        grid=(t // tm,),
        in_specs=[pl.BlockSpec((tm, k), lambda i: (i, 0)),
                  pl.BlockSpec((k, n), lambda i: (0, 0)),
                  pl.BlockSpec(gain.shape, lambda i: (0, 0)),
                  pl.BlockSpec((tm, LANES), lambda i: (i % nseq, 0)),
                  pl.BlockSpec((tm, LANES), lambda i: (i % nseq, 0)),
                  pl.BlockSpec((LANES, LANES), lambda i: (0, 0))],
        out_specs=pl.BlockSpec((None, dil, tm // dil, n), lambda i: (i // nseq, 0, i % nseq, 0)),
        out_shape=jax.ShapeDtypeStruct((batch, dil, seq // dil, n), BF16),
        scratch_shapes=[pltpu.VMEM((n // LANES, tm, LANES), F32)],
        compiler_params=_params("parallel"),
        name=f"proj_group_d{dil}",
    )(h, w, gain, cos_t, sin_t, seg)


def _band_kernel(q_ref, kp_ref, kc_ref, vp_ref, vc_ref, o_ref, lse_ref,
                 kk_ref, vv_ref, s_ref, p_ref, st_ref, bias_ref):
    i = pl.program_id(1)
    n_blk = q_ref.shape[0] // TQ
    n_pair = A_HEADS_PER_GROUP // 2
    kk_ref[0:TQ] = kp_ref[...]
    kk_ref[TQ:] = kc_ref[...]
    vv_ref[0:TQ] = vp_ref[...]
    vv_ref[TQ:] = vc_ref[...]
    row = lax.broadcasted_iota(I32, (TQ, 2 * TQ), 0)
    col = lax.broadcasted_iota(I32, (TQ, 2 * TQ), 1)
    dist = TQ + row - col
    band = (dist >= 0) & (dist <= A_BAND)
    bias_ref[0] = jnp.where(band & ((col >= TQ) | (i > 0)), 0.0, NEG)
    bias_ref[1] = jnp.where(band, 0.0, NEG)
    lane = lax.broadcasted_iota(I32, (TQ, LANES), 1)
    low = lane < HEAD_DIM

    def lanes(c):
        return slice(c * LANES, (c + 1) * LANES)

    def scores(j, p, slot):
        q2 = q_ref[j * TQ:(j + 1) * TQ, lanes(p)].astype(F32)
        qs = jnp.concatenate([jnp.where(low, q2, 0.0), jnp.where(low, 0.0, q2)], axis=0).astype(BF16)
        s_ref[slot] = _dot_nt(qs, kk_ref[j * TQ:(j + 2) * TQ, lanes(p)])

    def softmax(j, slot):
        for r0 in range(0, 2 * TQ, CH):
            rs, ts = slice(r0, r0 + CH), slice(r0 % TQ, r0 % TQ + CH)
            cols = [s_ref[slot, rs, lanes(c)] + bias_ref[min(j, 1), ts, lanes(c)] for c in range(2)]
            m = jnp.max(jnp.maximum(cols[0], cols[1]), axis=-1, keepdims=True)
            ps = [jnp.exp2(x - m) for x in cols]
            for c, pc in enumerate(ps):
                p_ref[slot, rs, lanes(c)] = pc.astype(BF16)
            st_ref[slot, 0, rs, :] = jnp.broadcast_to(m, (CH, LANES))
            st_ref[slot, 1, rs, :] = jnp.broadcast_to(jnp.sum(ps[0] + ps[1], axis=-1, keepdims=True), (CH, LANES))

    def values(j, p, slot, lse_tile):
        l = st_ref[slot, 1]
        o2 = _dot(p_ref[slot], vv_ref[j * TQ:(j + 2) * TQ, lanes(p)]) / l
        o_ref[j * TQ:(j + 1) * TQ, lanes(p)] = jnp.where(low, o2[:TQ], o2[TQ:])
        lse2 = st_ref[slot, 0] + jnp.log(l) * LOG2E
        return jnp.where(lane == 2 * p, lse2[:TQ], jnp.where(lane == 2 * p + 1, lse2[TQ:], lse_tile))

    units = [(j, p) for j in range(n_blk) for p in range(n_pair)]
    scores(*units[0], 0)
    lse_tile = jnp.zeros((TQ, LANES), F32)
    for u, (j, p) in enumerate(units):
        if u + 1 < len(units):
            scores(*units[u + 1], (u + 1) % 2)
        softmax(j, u % 2)
        lse_tile = values(j, p, u % 2, lse_tile)
        if p == n_pair - 1:
            lse_ref[j * TQ:(j + 1) * TQ, :] = lse_tile


def _band_attention(qkv, dil, batch, seq):
    sub = seq // dil
    nb = sub // TQ
    n_blk = min(BAND_BLOCKS, nb)
    rows = n_blk * TQ
    steps = nb // n_blk
    n_sub = batch * dil
    flat = qkv.reshape(n_sub * sub, qkv.shape[-1])

    def cur(n, i):
        return n * steps + i

    def prev(n, i):
        return n * nb + jnp.maximum(i * n_blk - 1, 0)

    o, lse = pl.pallas_call(
        _band_kernel,
        grid=(n_sub, steps),
        in_specs=[
            pl.BlockSpec((rows, A_OUT), lambda n, i: (cur(n, i), 0)),
            pl.BlockSpec((TQ, A_OUT), lambda n, i: (prev(n, i), 1)),
            pl.BlockSpec((rows, A_OUT), lambda n, i: (cur(n, i), 1)),
            pl.BlockSpec((TQ, A_OUT), lambda n, i: (prev(n, i), 2)),
            pl.BlockSpec((rows, A_OUT), lambda n, i: (cur(n, i), 2)),
        ],
        out_specs=[pl.BlockSpec((rows, A_OUT), lambda n, i: (cur(n, i), 0)),
                   pl.BlockSpec((rows, LANES), lambda n, i: (cur(n, i), 0))],
        out_shape=[jax.ShapeDtypeStruct((n_sub * sub, A_OUT), F32),
                   jax.ShapeDtypeStruct((n_sub * sub, LANES), F32)],
        scratch_shapes=[pltpu.VMEM((rows + TQ, A_OUT), BF16), pltpu.VMEM((rows + TQ, A_OUT), BF16),
                        pltpu.VMEM((2, 2 * TQ, 2 * TQ), F32), pltpu.VMEM((2, 2 * TQ, 2 * TQ), BF16),
                        pltpu.VMEM((2, 2, 2 * TQ, LANES), F32), pltpu.VMEM((2, TQ, 2 * TQ), F32)],
        compiler_params=_params("parallel", "parallel"),
        name=f"band_attention_d{dil}",
    )(flat, flat, flat, flat, flat)
    return o.reshape(batch, dil, sub, A_OUT), lse.reshape(batch, dil, sub, LANES)


def _compress_kernel(x_ref, pe_ref, w1_ref, w2_ref, gain_ref, cos_ref, sin_ref, seg_ref, o_ref):
    kv = pl.program_id(0)
    n = o_ref.shape[0]
    top = jnp.zeros((n, w1_ref.shape[-1]), F32)
    bot = jnp.zeros((n, w1_ref.shape[-1]), F32)
    for p in range(CMP_STRIDE):
        xp = x_ref[pl.ds(p, n, stride=CMP_STRIDE), :]
        top = top + _dot((xp + pe_ref[p:p + 1, :]).astype(BF16), w1_ref[p])
        bot = bot + _dot((xp + pe_ref[CMP_STRIDE + p:CMP_STRIDE + p + 1, :]).astype(BF16), w1_ref[CMP_STRIDE + p])
    hid = jax.nn.gelu(top + pltpu.roll(bot, n - 1, 0))
    out = _dot(hid.astype(BF16), w2_ref[...])

    @pl.when(kv == 0)
    def _():
        o_ref[...] = _head_norm_rope(out, gain_ref[...], cos_ref[...], sin_ref[...], seg_ref[...]).astype(o_ref.dtype)

    @pl.when(kv != 0)
    def _():
        o_ref[...] = out.astype(o_ref.dtype)


def _compress(fg, raw_block, pe, w1e, w2e, gain, cos_c, sin_c, seg, batch, seq):
    n_chunk = seq // CMP_STRIDE
    return pl.pallas_call(
        _compress_kernel,
        grid=(2, batch),
        in_specs=[
            pl.BlockSpec((seq, LANES), lambda kv, b: (b, raw_block + kv)),
            pl.BlockSpec((None,) + pe.shape[1:], lambda kv, b: (kv, 0, 0)),
            pl.BlockSpec((None,) + w1e.shape[1:], lambda kv, b: (kv, 0, 0, 0)),
            pl.BlockSpec((None,) + w2e.shape[1:], lambda kv, b: (kv, 0, 0)),
            pl.BlockSpec((1, LANES), lambda kv, b: (0, 0)),
            pl.BlockSpec((n_chunk, LANES), lambda kv, b: (0, 0)),
            pl.BlockSpec((n_chunk, LANES), lambda kv, b: (0, 0)),
            pl.BlockSpec((LANES, LANES), lambda kv, b: (0, 0)),
        ],
        out_specs=pl.BlockSpec((None, n_chunk, LANES), lambda kv, b: (kv, b, 0)),
        out_shape=jax.ShapeDtypeStruct((2, batch * n_chunk, LANES), BF16),
        compiler_params=_params("parallel", "parallel"),
        name="compress",
    )(fg, pe, w1e, w2e, gain, cos_c, sin_c, seg)


def _nsa_kernel(q_ref, ke_ref, vsel_ref, kwin_ref, vwin_ref, kc_ref, vc_ref, gate_ref, wslct_ref, o_ref,
                qa0, qa1, s0, s1, p0, p1, st0, st1, acc0, acc1, ot0, ot1, bias_ref, cmask_ref, imp_ref,
                *, n_cmp):
    qa, s_scr, p_scr, st = (qa0, qa1), (s0, s1), (p0, p1), (st0, st1)
    acc, ot = (acc0, acc1), (ot0, ot1)
    m_row, l_row, a_row = 0, 1, 2
    qi = pl.program_id(1)
    q0 = qi * TQ
    rows = B_GQA * TQ
    n_ck = kc_ref.shape[0]
    span = WIN + TQ
    n_full = q0 // SEL_TK
    w0 = pl.multiple_of(jnp.maximum(q0 - WIN, 0), TQ)
    lane = lax.broadcasted_iota(I32, (TQ, LANES), 1)
    low = lane < HEAD_DIM
    gates = jax.nn.sigmoid(gate_ref[...])

    def token_key(width):
        return (q0 + lax.broadcasted_iota(I32, (TQ, width), 0), lax.broadcasted_iota(I32, (TQ, width), 1))

    tr, cc = token_key(n_ck)
    cmp_valid = (cc * CMP_STRIDE + (CMP_LEN - 1) <= tr) & (cc < n_cmp)
    bias_ref[0, :, :n_ck] = jnp.where(cmp_valid, 0.0, NEG)
    cmask_ref[...] = jnp.where(cmp_valid, 1.0, 0.0)
    tr, cc = token_key(SEL_TK)
    bias_ref[1, :, :SEL_TK] = jnp.where(n_full * SEL_TK + cc <= tr, 0.0, NEG)
    tr, cc = token_key(span)
    dist = tr - (w0 + cc)
    bias_ref[2, :, :span] = jnp.where((dist >= 0) & (dist < WIN), 0.0, NEG)

    blk = lax.broadcasted_iota(I32, (LANES, TQ), 0)
    tok = q0 + lax.broadcasted_iota(I32, (LANES, TQ), 1)
    cur = jnp.right_shift(tok, SEL_SHIFT)
    forced = (blk == 0) | (blk == cur) | (blk == cur - 1)
    causal_blk = blk * SEL_LEN <= tok
    blk_f = blk.astype(F32)

    def gate_col(branch, h):
        base = branch * B_Q_HEADS + h * B_GQA
        return jnp.concatenate([gates[:, base + g:base + g + 1] for g in range(B_GQA)], axis=0)

    def lanes(c):
        return slice(c * LANES, (c + 1) * LANES)

    def score_cols(h, r0, width, mask):
        ts = r0 % TQ
        cols = [s_scr[h][r0:r0 + CH, lanes(c)] for c in range(width // LANES)]
        if mask is not None:
            cols = [x + bias_ref[mask, ts:ts + CH, lanes(c)] for c, x in enumerate(cols)]
        return cols

    def row_max(cols):
        mx = cols[0]
        for x in cols[1:]:
            mx = jnp.maximum(mx, x)
        return jnp.max(mx, axis=-1, keepdims=True)

    def row_sum(cols):
        tot = cols[0]
        for x in cols[1:]:
            tot = tot + x
        return jnp.sum(tot, axis=-1, keepdims=True)

    def softmax_cmp(h):
        for r0 in range(0, rows, CH):
            ts = r0 % TQ
            cols = score_cols(h, r0, n_ck, 0)
            m = row_max(cols)
            es = [jnp.exp2(x - m) * cmask_ref[ts:ts + CH, lanes(c)] for c, x in enumerate(cols)]
            rinv = 1.0 / jnp.maximum(row_sum(es), TINY)
            for c, e in enumerate(es):
                pn = e * rinv
                p_scr[h][r0:r0 + CH, lanes(c)] = pn.astype(BF16)
                if r0 < TQ:
                    imp_ref[h, ts:ts + CH, lanes(c)] = pn
                else:
                    imp_ref[h, ts:ts + CH, lanes(c)] += pn

    def softmax_online(h, mask):
        for r0 in range(0, rows, CH):
            rs = slice(r0, r0 + CH)
            cols = score_cols(h, r0, SEL_TK, mask)
            m_old = st[h][m_row, rs, :]
            m_new = jnp.maximum(m_old, row_max(cols))
            alpha = jnp.exp2(m_old - m_new)
            ps = [jnp.exp2(x - m_new) for x in cols]
            for c, pc in enumerate(ps):
                p_scr[h][rs, lanes(c)] = pc.astype(BF16)
            st[h][l_row, rs, :] = alpha * st[h][l_row, rs, :] + row_sum(ps)
            st[h][m_row, rs, :] = m_new
            st[h][a_row, rs, :] = alpha

    def softmax_once(h, width, mask):
        for r0 in range(0, rows, CH):
            rs = slice(r0, r0 + CH)
            cols = score_cols(h, r0, width, mask)
            m = row_max(cols)
            ps = [jnp.exp2(x - m) for x in cols]
            for c, pc in enumerate(ps):
                p_scr[h][rs, lanes(c)] = pc.astype(BF16)
            st[h][l_row, rs, :] = jnp.broadcast_to(row_sum(ps), (CH, LANES))

    def score_tile(h, kt):
        k0 = pl.multiple_of(kt * SEL_TK, SEL_TK)
        s_scr[h][:, :SEL_TK] = _dot_nt(qa[h][...], ke_ref[pl.ds(k0, SEL_TK), :])

    def value_tile(h, kt, live=None):
        k0 = pl.multiple_of(kt * SEL_TK, SEL_TK)
        pv = _dot(p_scr[h][:, :SEL_TK], vsel_ref[pl.ds(k0, SEL_TK), :])
        if live is not None:
            pv = jnp.where(live, pv, 0.0)
        acc[h][...] = st[h][a_row] * acc[h][...] + pv

    for h in range(B_KV_HEADS):
        for g in range(B_GQA):
            c, half = (h * B_GQA + g) // 2, g % 2
            chunk = q_ref[:, lanes(c)].astype(F32)
            if half != h:
                chunk = pltpu.roll(chunk, HEAD_DIM, 1)
            chunk = jnp.where(low if h == 0 else jnp.logical_not(low), chunk, 0.0)
            qa[h][g * TQ:(g + 1) * TQ, :LANES] = chunk.astype(BF16)
        s_scr[h][:, :n_ck] = _dot_nt(qa[h][:, :LANES], kc_ref[...])

    for h in range(B_KV_HEADS):
        softmax_cmp(h)
        ot[h][...] = gate_col(0, h) * _dot(p_scr[h][:, :n_ck], vc_ref[...])

        p_slc = _dot_nt(wslct_ref[...], _split(imp_ref[h]))
        score = jnp.where(forced, FORCE_SCORE, jnp.where(causal_blk, p_slc, -1.0))
        for _ in range(N_SELECT):
            best = jnp.max(score, axis=0, keepdims=True)
            first = jnp.min(jnp.where(score == best, blk_f, float(LANES)), axis=0, keepdims=True)
            score = jnp.where(blk_f == first, -2.0, score)
        bias = jnp.where(score == -2.0, 0.0, NEG).T.astype(BF16)
        qa[h][:, LANES:] = jnp.concatenate([bias] * B_GQA, axis=0)

        st[h][m_row] = jnp.full((rows, LANES), NEG, F32)
        st[h][l_row] = jnp.zeros((rows, LANES), F32)
        st[h][a_row] = jnp.ones((rows, LANES), F32)
        acc[h][...] = jnp.zeros((rows, LANES), F32)

    score_tile(0, 0)

    def sweep(kt, carry):
        value_tile(1, jnp.maximum(kt - 1, 0), live=kt > 0)
        score_tile(1, kt)
        softmax_online(0, None)
        value_tile(0, kt)
        score_tile(0, kt + 1)
        softmax_online(1, None)
        return carry

    lax.fori_loop(0, n_full, sweep, 0)
    value_tile(1, jnp.maximum(n_full - 1, 0), live=n_full > 0)
    score_tile(1, n_full)
    softmax_online(0, 1)
    value_tile(0, n_full)
    softmax_online(1, 1)
    value_tile(1, n_full)

    kw = kwin_ref[pl.ds(w0, span), :]
    vw = vwin_ref[pl.ds(w0, span), :]
    for h in range(B_KV_HEADS):
        acc[h][...] = acc[h][...] / st[h][l_row]
        s_scr[h][:, :span] = _dot_nt(qa[h][:, :LANES], kw)
    for h in range(B_KV_HEADS):
        softmax_once(h, span, 2)
        o_win = _dot(p_scr[h][:, :span], vw) / st[h][l_row]
        o_tot = ot[h][...] + gate_col(1, h) * acc[h][...] + gate_col(2, h) * o_win

        for c2 in range(B_GQA // 2):
            halves = []
            for half in range(2):
                g = 2 * c2 + half
                piece = o_tot[g * TQ:(g + 1) * TQ]
                halves.append(piece if half == h else pltpu.roll(piece, HEAD_DIM, 1))
            o_ref[:, lanes(h * (B_GQA // 2) + c2)] = jnp.where(low, halves[0], halves[1]).astype(o_ref.dtype)


def _nsa(bqk, ke, bv, cmp_kv, fgate, wslct2, batch, seq, gate_block):
    t = bqk.shape[0]
    nq = seq // TQ
    n_ck = cmp_kv.shape[1] // batch
    n_cmp = seq // CMP_STRIDE - CMP_LEN // CMP_STRIDE + 1
    rows = B_GQA * TQ
    width = max(WIN + TQ, SEL_TK, n_ck)
    return pl.pallas_call(
        functools.partial(_nsa_kernel, n_cmp=n_cmp),
        grid=(batch, nq),
        in_specs=[
            pl.BlockSpec((TQ, B_QD), lambda b, i: (b * nq + i, 0)),
            pl.BlockSpec((seq, 2 * LANES), lambda b, i: (b, 0)),
            pl.BlockSpec((seq, LANES), lambda b, i: (b, 0)),
            pl.BlockSpec((seq, LANES), lambda b, i: (b, B_QD // LANES + 1)),
            pl.BlockSpec((seq, LANES), lambda b, i: (b, 1)),
            pl.BlockSpec((None, n_ck, LANES), lambda b, i: (0, b, 0)),
            pl.BlockSpec((None, n_ck, LANES), lambda b, i: (1, b, 0)),
            pl.BlockSpec((TQ, LANES), lambda b, i: (b * nq + i, gate_block)),
            pl.BlockSpec(wslct2.shape, lambda b, i: (0, 0)),
        ],
        out_specs=pl.BlockSpec((TQ, B_OUT), lambda b, i: (b * nq + i, 0)),
        out_shape=jax.ShapeDtypeStruct((t, B_OUT), BF16),
        scratch_shapes=(
            [pltpu.VMEM((rows, 2 * LANES), BF16)] * 2 +
            [pltpu.VMEM((rows, width), F32)] * 2 +
            [pltpu.VMEM((rows, width), BF16)] * 2 +
            [pltpu.VMEM((3, rows, LANES), F32)] * 2 +
            [pltpu.VMEM((rows, LANES), F32)] * 4 +
            [pltpu.VMEM((3, TQ, width), F32), pltpu.VMEM((TQ, n_ck), F32), pltpu.VMEM((2, TQ, n_ck), F32)]),
        compiler_params=_params("parallel", "parallel"),
        name="nsa_mixer",
    )(bqk, ke, bv, bqk, bv, cmp_kv, cmp_kv, fgate, wslct2)


def _merge_kernel(o0_ref, o1_ref, o2_ref, l0_ref, l1_ref, l2_ref, ob_ref, m0_ref, m1_ref, x_ref,
                  wpa_ref, wpb_ref, wout_ref, exp_ref, g_ref, xo_ref, h_ref, osc_ref, lsc_ref, oa_ref, mix_ref):
    def token_order(ref, scr, slot):
        d, n, width = ref.shape
        if d == 1:
            return ref[0]
        chunks = width // LANES
        for c in range(chunks):
            for r in range(d):
                scr[slot * chunks + c, pl.ds(r, n, stride=d), :] = ref[r, :, c * LANES:(c + 1) * LANES]
        return jnp.concatenate([scr[slot * chunks + c] for c in range(chunks)], axis=1)

    l0, l1, l2 = token_order(l0_ref, lsc_ref, 0), token_order(l1_ref, lsc_ref, 0), token_order(l2_ref, lsc_ref, 1)
    o0, o1, o2 = token_order(o0_ref, osc_ref, 0), token_order(o1_ref, osc_ref, 0), token_order(o2_ref, osc_ref, 1)
    mx = jnp.maximum(jnp.maximum(l0, l1), l2)
    e0, e1, e2 = jnp.exp2(l0 - mx), jnp.exp2(l1 - mx), jnp.exp2(l2 - mx)
    den = e0 + e1 + e2
    ex = exp_ref[...]
    out_a = _split_dot(e0 / den, ex) * o0 + _split_dot(e1 / den, ex) * o1 + _split_dot(e2 / den, ex) * o2
    oa_ref[...] = out_a.astype(BF16)
    d = x_ref.shape[1]
    for c0 in range(0, d, MXU_COLS):
        sl = slice(c0, c0 + MXU_COLS)
        pa = _dot(oa_ref[...], wpa_ref[:, sl])
        pb = _dot(ob_ref[...], wpb_ref[:, sl])
        mixed = jax.nn.sigmoid(m0_ref[:, sl]) * pa + jax.nn.sigmoid(m1_ref[:, sl]) * pb
        mix_ref[:, sl] = mixed.astype(BF16)
    ssq = None
    for c0 in range(0, d, MXU_COLS):
        sl = slice(c0, c0 + MXU_COLS)
        x = x_ref[:, sl] + _dot(mix_ref[...], wout_ref[:, sl])
        xo_ref[:, sl] = x
        part = jnp.sum(x * x, axis=-1, keepdims=True)
        ssq = part if ssq is None else ssq + part
    scale = lax.rsqrt(ssq * (1.0 / d) + EPS)
    for c0 in range(0, d, MXU_COLS):
        sl = slice(c0, c0 + MXU_COLS)
        h_ref[:, sl] = (xo_ref[:, sl] * scale * g_ref[:, sl]).astype(h_ref.dtype)


def _merge(o_groups, lse_groups, out_b, fgate, x, wpa, wpb, wout, expand2, g_next, seq, tm=512):
    t, d = x.shape
    nseq = seq // tm
    row = lambda w: pl.BlockSpec((tm, w), lambda i: (i, 0))
    full = lambda a: pl.BlockSpec(a.shape, lambda i: (0,) * a.ndim)
    grouped = lambda a: pl.BlockSpec((None, a.shape[1], tm // a.shape[1], a.shape[3]),
                                     lambda i: (i // nseq, 0, i % nseq, 0))
    g_next = g_next.reshape(1, d)
    return pl.pallas_call(
        _merge_kernel,
        grid=(t // tm,),
        in_specs=[grouped(a) for a in o_groups] + [grouped(a) for a in lse_groups] + [
            row(B_OUT), pl.BlockSpec((tm, d), lambda i: (i, 0)), pl.BlockSpec((tm, d), lambda i: (i, 1)), row(d),
            full(wpa), full(wpb), full(wout), full(expand2), full(g_next)],
        out_specs=[row(d), row(d)],
        out_shape=[jax.ShapeDtypeStruct((t, d), F32), jax.ShapeDtypeStruct((t, d), BF16)],
        scratch_shapes=[pltpu.VMEM((2 * A_OUT // LANES, tm, LANES), F32), pltpu.VMEM((2, tm, LANES), F32),
                        pltpu.VMEM((tm, A_OUT), BF16), pltpu.VMEM((tm, d), BF16)],
        compiler_params=_params("parallel"),
        name="merge_out",
    )(*o_groups, *lse_groups, out_b, fgate, fgate, x, wpa, wpb, wout, expand2, g_next)


def _ffn_up_kernel(h_ref, wg_ref, wu_ref, cw_ref, cb_ref, o_ref, gbuf_ref, *, tiles_per_seq):
    i = pl.program_id(1)
    tm = h_ref.shape[0]
    h = h_ref[...]

    @pl.when(i % tiles_per_seq == 0)
    def _():
        gbuf_ref[0:8, :] = jnp.zeros((8, gbuf_ref.shape[1]), F32)

    @pl.when(i % tiles_per_seq != 0)
    def _():
        gbuf_ref[0:8, :] = gbuf_ref[tm:tm + 8, :]

    gbuf_ref[8:tm + 8, :] = _dot(h, wg_ref[...])
    u = _dot(h, wu_ref[...])
    conv = cb_ref[...] + cw_ref[0:1, :] * gbuf_ref[pl.ds(6, tm), :]
    conv = conv + cw_ref[1:2, :] * gbuf_ref[pl.ds(7, tm), :]
    conv = conv + cw_ref[2:3, :] * gbuf_ref[pl.ds(8, tm), :]
    o_ref[...] = (jax.nn.silu(conv) * u).astype(o_ref.dtype)


def _ffn_up(h, w_up, conv_w, conv_b, seq, tm=512, tn=1408):
    t, d = h.shape
    nj = D_FF // tn
    return pl.pallas_call(
        functools.partial(_ffn_up_kernel, tiles_per_seq=seq // tm),
        grid=(nj, t // tm),
        in_specs=[
            pl.BlockSpec((tm, d), lambda j, i: (i, 0)),
            pl.BlockSpec((d, tn), lambda j, i: (0, j)),
            pl.BlockSpec((d, tn), lambda j, i: (0, nj + j)),
            pl.BlockSpec((CONV_W, tn), lambda j, i: (0, j)),
            pl.BlockSpec((1, tn), lambda j, i: (0, j)),
        ],
        out_specs=pl.BlockSpec((tm, tn), lambda j, i: (i, j)),
        out_shape=jax.ShapeDtypeStruct((t, D_FF), BF16),
        scratch_shapes=[pltpu.VMEM((tm + 8, tn), F32)],
        compiler_params=_params("arbitrary", "arbitrary"),
        name="ffn_up_conv",
    )(h, w_up, w_up, conv_w, conv_b.reshape(1, D_FF))


def _ffn_down_kernel(a_ref, w_ref, x_ref, g_ref, xo_ref, h_ref):
    x = x_ref[...] + _dot(a_ref[...], w_ref[...])
    xo_ref[...] = x
    y = x * lax.rsqrt(jnp.mean(x * x, axis=-1, keepdims=True) + EPS)
    h_ref[...] = (y * g_ref[...]).astype(h_ref.dtype)


def _ffn_down(act, w_down, x, g_next, tm=512):
    t, d = x.shape
    row = lambda w: pl.BlockSpec((tm, w), lambda i: (i, 0))
    return pl.pallas_call(
        _ffn_down_kernel,
        grid=(t // tm,),
        in_specs=[row(D_FF), pl.BlockSpec(w_down.shape, lambda i: (0, 0)), row(d),
                  pl.BlockSpec((1, d), lambda i: (0, 0))],
        out_specs=[row(d), row(d)],
        out_shape=[jax.ShapeDtypeStruct((t, d), F32), jax.ShapeDtypeStruct((t, d), BF16)],
        compiler_params=_params("parallel"),
        name="ffn_down",
    )(act, w_down, x, g_next.reshape(1, d))


def _rope_tables(pos):
    half = HEAD_DIM // 2
    inv_freq = ROPE_THETA ** (-jnp.arange(half, dtype=F32) / half)
    ang = pos.astype(F32)[:, None] * inv_freq[None, :]
    c, s = jnp.cos(ang), jnp.sin(ang)
    return jnp.tile(c, (1, 4)), jnp.tile(jnp.concatenate([-s, s], axis=1), (1, 2))


def _constants(n_ck, seq):
    lane = np.arange(LANES)
    seg = (lane[:, None] // HEAD_DIM == lane[None, :] // HEAD_DIM).astype(np.float32)
    seg = seg / HEAD_DIM
    ratio = SEL_LEN // CMP_STRIDE
    wslc = np.zeros((n_ck, LANES), np.float32)
    for off in range(CMP_LEN // CMP_STRIDE):
        for i in range(n_ck):
            if i + off < n_ck:
                wslc[i, (i + off) // ratio] += 1.0
    wslct2 = np.concatenate([wslc, wslc], axis=0).T
    expand = np.zeros((LANES, A_OUT), np.float32)
    for j in range(A_HEADS_PER_GROUP):
        expand[j, j * HEAD_DIM:(j + 1) * HEAD_DIM] = 1.0
    expand2 = np.concatenate([expand, expand], axis=0)
    onehot = (np.arange(seq)[:, None] // SEL_LEN == lane[None, :]).astype(np.float32)
    return (jnp.asarray(seg, BF16), jnp.asarray(wslct2, BF16), jnp.asarray(expand2, BF16),
            jnp.asarray(onehot, BF16))


def kernel(x, norm_mix_g, w_in, a_q_g, a_k_g, b_q_g, b_k_g, cmp_pos, cmp_w1, cmp_w2, w_proj_a, w_proj_b,
           w_out, norm_ffn_g, w_up, conv_w, conv_b, w_down):
    batch, seq, d = x.shape
    depth = w_in.shape[0]
    t = batch * seq
    assert d == D_MODEL and seq % (16 * TQ) == 0 and seq // SEL_LEN <= LANES
    n_ck = seq // CMP_STRIDE
    seg, wslct2, expand2, onehot = _constants(n_ck, seq)
    onehot = jnp.tile(onehot, (batch, 1))
    cos_t, sin_t = _rope_tables(jnp.arange(seq))
    cos_c, sin_c = _rope_tables(jnp.arange(n_ck) * CMP_STRIDE + CMP_LEN - 1)
    eye_h = jnp.eye(B_KV_HEADS, dtype=F32)
    o_kq, o_kv, o_gate, o_merge = A_QKV, A_QKV + B_QD, A_QKV + B_QD + B_KV, A_QKV + B_QD + B_KV + B_GATE
    n_a = A_HEADS * HEAD_DIM
    f_gate_block = 2 * D_MODEL // LANES

    xf = x.reshape(t, d)
    h = _rmsnorm(xf, norm_mix_g[0])
    for l in range(depth):
        w = w_in[l]
        kv = lambda i: w[:, o_kv + i * LANES:o_kv + (i + 1) * LANES]
        w_bqk = jnp.concatenate([w[:, o_kq:o_kv], kv(2), kv(4)], axis=1).astype(BF16)
        w_bv = jnp.concatenate([kv(3), kv(5)], axis=1).astype(BF16)
        w_f = jnp.concatenate([w[:, o_merge:], w[:, o_gate:o_merge],
                               jnp.zeros((d, LANES - B_GATE), F32), kv(0), kv(1)], axis=1).astype(BF16)
        gain_a = jnp.concatenate([jnp.tile(a_q_g[l] * Q_SCALE, A_HEADS_PER_GROUP),
                                  jnp.tile(a_k_g[l], A_HEADS_PER_GROUP)]).reshape(1, -1)
        gain_b = jnp.concatenate([jnp.tile(b_q_g[l] * Q_SCALE, B_Q_HEADS), jnp.tile(b_k_g[l, 1], B_KV_HEADS),
                                  jnp.tile(b_k_g[l, 2], B_KV_HEADS)]).reshape(1, -1)

        o_groups, lse_groups = [], []
        for gi, (_, dil) in enumerate(DIL_PAIRS):
            cols = [w[:, part * n_a + gi * A_OUT:part * n_a + (gi + 1) * A_OUT] for part in range(3)]
            qkv = _proj_group(h, jnp.concatenate(cols, axis=1).astype(BF16), gain_a, cos_t, sin_t, seg,
                              dil, batch, seq)
            o_g, lse_g = _band_attention(qkv, dil, batch, seq)
            o_groups.append(o_g)
            lse_groups.append(lse_g)

        bqk = _proj(h, w_bqk, BF16, tn=768, rope=(gain_b, cos_t, sin_t, seg), seq=seq)
        bv = _proj(h, w_bv, BF16, tn=256)
        fg = _proj(h, w_f, F32, tn=w_f.shape[1], tm=256)
        pe = jnp.tile(cmp_pos[l], (1, 1, B_KV_HEADS))
        w1 = cmp_w1[l].reshape(2, CMP_LEN, HEAD_DIM, CMP_HIDDEN)
        w1e = jnp.einsum('kpdn,hj->kphdjn', w1, eye_h).reshape(2, CMP_LEN, LANES, 2 * CMP_HIDDEN)
        w2e = jnp.einsum('knd,hj->khnjd', cmp_w2[l], eye_h).reshape(2, 2 * CMP_HIDDEN, LANES)
        gain_c = jnp.tile(b_k_g[l, 0], B_KV_HEADS).reshape(1, LANES)
        cmp_kv = _compress(fg, f_gate_block + 1, pe, w1e.astype(BF16), w2e.astype(BF16), gain_c,
                           cos_c, sin_c, seg, batch, seq)
        ke = jnp.concatenate([bqk[:, B_QD:B_QD + LANES], onehot], axis=1)
        out_b = _nsa(bqk, ke, bv, cmp_kv, fg, wslct2, batch, seq, f_gate_block)

        xf, h = _merge(o_groups, lse_groups, out_b, fg, xf, w_proj_a[l].astype(BF16), w_proj_b[l].astype(BF16),
                       w_out[l].astype(BF16), expand2, norm_ffn_g[l], seq)
        act = _ffn_up(h, w_up[l].astype(BF16), conv_w[l], conv_b[l], seq)
        g_next = norm_mix_g[l + 1] if l + 1 < depth else jnp.ones((d,), F32)
        xf, h = _ffn_down(act, w_down[l].astype(BF16), xf, g_next)
    return xf.reshape(batch, seq, d)
```

```python
import functools

import numpy as np
import jax
import jax.numpy as jnp
from jax import lax
from jax.experimental import pallas as pl
from jax.experimental.pallas import tpu as pltpu

F32 = jnp.float32
BF16 = jnp.bfloat16
I32 = jnp.int32

D_MODEL = 1024
HEAD_DIM = 64
ROPE_THETA = 10000.0
EPS = 1e-6
NEG = -1e30
TINY = 1e-30
FORCE_SCORE = 1e9

DIL_PAIRS = ((128, 1), (512, 4), (2048, 16))
A_HEADS_PER_GROUP = 8
A_HEADS = 24
A_OUT = 512
A_BAND = 128

B_Q_HEADS = 8
B_KV_HEADS = 2
B_GQA = 4
CMP_LEN = 32
CMP_STRIDE = 16
CMP_HIDDEN = 256
SEL_LEN = 64
SEL_SHIFT = 6
N_SELECT = 16
WIN = 512
B_OUT = 512

D_FF = 2816
CONV_W = 3

A_QKV = 3 * A_HEADS * HEAD_DIM
B_QD = B_Q_HEADS * HEAD_DIM
B_KV = 3 * 2 * B_KV_HEADS * HEAD_DIM
B_GATE = 3 * B_Q_HEADS

LOG2E = 1.4426950408889634
Q_SCALE = LOG2E * HEAD_DIM ** -0.5

LANES = 128
MXU_COLS = 256
TQ = 128
BAND_BLOCKS = 4
SEL_TK = 512
CH = 32
VMEM_LIMIT = 48 * 1024 * 1024

_NT = (((1,), (1,)), ((), ()))


def _dot(a, b):
    return jnp.dot(a, b, preferred_element_type=F32)


def _dot_nt(a, b):
    return lax.dot_general(a, b, _NT, preferred_element_type=F32)


def _split(x):
    hi = x.astype(BF16)
    lo = (x - hi.astype(F32)).astype(BF16)
    return jnp.concatenate([hi, lo], axis=1)


def _split_dot(x, w2):
    return _dot(_split(x), w2)


def _params(*sem):
    return pltpu.CompilerParams(dimension_semantics=sem, vmem_limit_bytes=VMEM_LIMIT)


def _rmsnorm_kernel(x_ref, g_ref, h_ref):
    x = x_ref[...]
    y = x * lax.rsqrt(jnp.mean(x * x, axis=-1, keepdims=True) + EPS)
    h_ref[...] = (y * g_ref[...]).astype(h_ref.dtype)


def _rmsnorm(x, g, tm=512):
    t, d = x.shape
    return pl.pallas_call(
        _rmsnorm_kernel,
        grid=(t // tm,),
        in_specs=[pl.BlockSpec((tm, d), lambda i: (i, 0)), pl.BlockSpec((1, d), lambda i: (0, 0))],
        out_specs=pl.BlockSpec((tm, d), lambda i: (i, 0)),
        out_shape=jax.ShapeDtypeStruct((t, d), BF16),
        compiler_params=_params("parallel"),
        name="rmsnorm",
    )(x, g.reshape(1, d))


def _head_norm_rope(y, gain, cos_t, sin_t, seg):
    ms = _dot((y * y).astype(BF16), seg)
    yn = y * lax.rsqrt(ms + EPS) * gain
    return yn * cos_t + pltpu.roll(yn, HEAD_DIM, 1) * sin_t


def _proj_kernel(h_ref, w_ref, o_ref):
    o_ref[...] = _dot(h_ref[...], w_ref[...]).astype(o_ref.dtype)


def _proj_rope_kernel(h_ref, w_ref, gain_ref, cos_ref, sin_ref, seg_ref, o_ref):
    y = _dot(h_ref[...], w_ref[...])
    cos_t, sin_t, seg = cos_ref[...], sin_ref[...], seg_ref[...]
    for c in range(y.shape[1] // LANES):
        sl = slice(c * LANES, (c + 1) * LANES)
        o_ref[:, sl] = _head_norm_rope(y[:, sl], gain_ref[:, sl], cos_t, sin_t, seg).astype(o_ref.dtype)


def _proj(h, w, out_dtype, tn, tm=512, rope=None, seq=None):
    t, k = h.shape
    n = w.shape[1]
    grid = (n // tn, t // tm)
    h_spec = pl.BlockSpec((tm, k), lambda j, i: (i, 0))
    w_spec = pl.BlockSpec((k, tn), lambda j, i: (0, j))
    o_spec = pl.BlockSpec((tm, tn), lambda j, i: (i, j))
    out_shape = jax.ShapeDtypeStruct((t, n), out_dtype)
    if rope is None:
        return pl.pallas_call(_proj_kernel, grid=grid, in_specs=[h_spec, w_spec], out_specs=o_spec,
                              out_shape=out_shape, compiler_params=_params("parallel", "parallel"),
                              name="proj")(h, w)
    gain, cos_t, sin_t, seg = rope
    nseq = seq // tm
    return pl.pallas_call(
        _proj_rope_kernel, grid=grid,
        in_specs=[h_spec, w_spec,
                  pl.BlockSpec((1, tn), lambda j, i: (0, j)),
                  pl.BlockSpec((tm, LANES), lambda j, i: (i % nseq, 0)),
                  pl.BlockSpec((tm, LANES), lambda j, i: (i % nseq, 0)),
                  pl.BlockSpec((LANES, LANES), lambda j, i: (0, 0))],
        out_specs=o_spec, out_shape=out_shape,
        compiler_params=_params("parallel", "parallel"), name="proj_rope",
    )(h, w, gain, cos_t, sin_t, seg)


def _proj_group_kernel(h_ref, w_ref, gain_ref, cos_ref, sin_ref, seg_ref, o_ref, y_ref, *, n_rope):
    dil, n_sub, _ = o_ref.shape
    y = _dot(h_ref[...], w_ref[...])
    cos_t, sin_t, seg = cos_ref[...], sin_ref[...], seg_ref[...]
    for c in range(y.shape[1] // LANES):
        sl = slice(c * LANES, (c + 1) * LANES)
        val = _head_norm_rope(y[:, sl], gain_ref[:, sl], cos_t, sin_t, seg) if c < n_rope else y[:, sl]
        if dil == 1:
            o_ref[0, :, sl] = val.astype(o_ref.dtype)
        else:
            y_ref[c] = val
            for r in range(dil):
                o_ref[r, :, sl] = y_ref[c, pl.ds(r, n_sub, stride=dil), :].astype(o_ref.dtype)


def _proj_group(h, w, gain, cos_t, sin_t, seg, dil, batch, seq, tm=512):
    t, k = h.shape
    n = w.shape[1]
    nseq = seq // tm
    return pl.pallas_call(
        functools.partial(_proj_group_kernel, n_rope=gain.shape[1] // LANES),
        grid=(t // tm,),
        in_specs=[pl.BlockSpec((tm, k), lambda i: (i, 0)),
                  pl.BlockSpec((k, n), lambda i: (0, 0)),
                  pl.BlockSpec(gain.shape, lambda i: (0, 0)),
                  pl.BlockSpec((tm, LANES), lambda i: (i % nseq, 0)),
                  pl.BlockSpec((tm, LANES), lambda i: (i % nseq, 0)),
                  pl.BlockSpec((LANES, LANES), lambda i: (0, 0))],
        out_specs=pl.BlockSpec((None, dil, tm // dil, n), lambda i: (i // nseq, 0, i % nseq, 0)),
        out_shape=jax.ShapeDtypeStruct((batch, dil, seq // dil, n), BF16),
        scratch_shapes=[pltpu.VMEM((n // LANES, tm, LANES), F32)],
        compiler_params=_params("parallel"),
        name=f"proj_group_d{dil}",
    )(h, w, gain, cos_t, sin_t, seg)


def _band_kernel(q_ref, kp_ref, kc_ref, vp_ref, vc_ref, o_ref, lse_ref,
                 kk_ref, vv_ref, s_ref, p_ref, st_ref, bias_ref):
    i = pl.program_id(1)
    n_blk = q_ref.shape[0] // TQ
    n_pair = A_HEADS_PER_GROUP // 2
    kk_ref[0:TQ] = kp_ref[...]
    kk_ref[TQ:] = kc_ref[...]
    vv_ref[0:TQ] = vp_ref[...]
    vv_ref[TQ:] = vc_ref[...]
    row = lax.broadcasted_iota(I32, (TQ, 2 * TQ), 0)
    col = lax.broadcasted_iota(I32, (TQ, 2 * TQ), 1)
    dist = TQ + row - col
    band = (dist >= 0) & (dist <= A_BAND)
    bias_ref[0] = jnp.where(band & ((col >= TQ) | (i > 0)), 0.0, NEG)
    bias_ref[1] = jnp.where(band, 0.0, NEG)
    lane = lax.broadcasted_iota(I32, (TQ, LANES), 1)
    low = lane < HEAD_DIM
    first = (lane & (HEAD_DIM // 2)) == 0

    def lanes(c):
        return slice(c * LANES, (c + 1) * LANES)

    def scores(j, p, slot):
        q2 = q_ref[j * TQ:(j + 1) * TQ, lanes(p)].astype(F32)
        qs = jnp.concatenate([jnp.where(first, q2, 0.0), jnp.where(first, 0.0, q2)], axis=0).astype(BF16)
        s_ref[slot] = _dot_nt(qs, kk_ref[j * TQ:(j + 2) * TQ, lanes(p)])

    def softmax(j, slot):
        for r0 in range(0, 2 * TQ, CH):
            rs, ts = slice(r0, r0 + CH), slice(r0 % TQ, r0 % TQ + CH)
            cols = [s_ref[slot, rs, lanes(c)] + bias_ref[min(j, 1), ts, lanes(c)] for c in range(2)]
            m = jnp.max(jnp.maximum(cols[0], cols[1]), axis=-1, keepdims=True)
            ps = [jnp.exp2(x - m) for x in cols]
            for c, pc in enumerate(ps):
                p_ref[slot, rs, lanes(c)] = pc.astype(BF16)
            st_ref[slot, 0, rs, :] = jnp.broadcast_to(m, (CH, LANES))
            st_ref[slot, 1, rs, :] = jnp.broadcast_to(jnp.sum(ps[0] + ps[1], axis=-1, keepdims=True), (CH, LANES))

    def values(j, p, slot, lse_tile):
        l = st_ref[slot, 1]
        o2 = _dot(p_ref[slot], vv_ref[j * TQ:(j + 2) * TQ, lanes(p)]) / l
        o_ref[j * TQ:(j + 1) * TQ, lanes(p)] = jnp.where(low, o2[:TQ], o2[TQ:])
        lse2 = st_ref[slot, 0] + jnp.log(l) * LOG2E
        return jnp.where(lane == 2 * p, lse2[:TQ], jnp.where(lane == 2 * p + 1, lse2[TQ:], lse_tile))

    units = [(j, p) for j in range(n_blk) for p in range(n_pair)]
    scores(*units[0], 0)
    lse_tile = jnp.zeros((TQ, LANES), F32)
    for u, (j, p) in enumerate(units):
        if u + 1 < len(units):
            scores(*units[u + 1], (u + 1) % 2)
        softmax(j, u % 2)
        lse_tile = values(j, p, u % 2, lse_tile)
        if p == n_pair - 1:
            lse_ref[j * TQ:(j + 1) * TQ, :] = lse_tile


def _band_attention(qkv, dil, batch, seq):
    sub = seq // dil
    nb = sub // TQ
    n_blk = min(BAND_BLOCKS, nb)
    rows = n_blk * TQ
    steps = nb // n_blk
    n_sub = batch * dil
    flat = qkv.reshape(n_sub * sub, qkv.shape[-1])

    def cur(n, i):
        return n * steps + i

    def prev(n, i):
        return n * nb + jnp.maximum(i * n_blk - 1, 0)

    o, lse = pl.pallas_call(
        _band_kernel,
        grid=(n_sub, steps),
        in_specs=[
            pl.BlockSpec((rows, A_OUT), lambda n, i: (cur(n, i), 0)),
            pl.BlockSpec((TQ, A_OUT), lambda n, i: (prev(n, i), 1)),
            pl.BlockSpec((rows, A_OUT), lambda n, i: (cur(n, i), 1)),
            pl.BlockSpec((TQ, A_OUT), lambda n, i: (prev(n, i), 2)),
            pl.BlockSpec((rows, A_OUT), lambda n, i: (cur(n, i), 2)),
        ],
        out_specs=[pl.BlockSpec((rows, A_OUT), lambda n, i: (cur(n, i), 0)),
                   pl.BlockSpec((rows, LANES), lambda n, i: (cur(n, i), 0))],
        out_shape=[jax.ShapeDtypeStruct((n_sub * sub, A_OUT), F32),
                   jax.ShapeDtypeStruct((n_sub * sub, LANES), F32)],
        scratch_shapes=[pltpu.VMEM((rows + TQ, A_OUT), BF16), pltpu.VMEM((rows + TQ, A_OUT), BF16),
                        pltpu.VMEM((2, 2 * TQ, 2 * TQ), F32), pltpu.VMEM((2, 2 * TQ, 2 * TQ), BF16),
                        pltpu.VMEM((2, 2, 2 * TQ, LANES), F32), pltpu.VMEM((2, TQ, 2 * TQ), F32)],
        compiler_params=_params("parallel", "parallel"),
        name=f"band_attention_d{dil}",
    )(flat, flat, flat, flat, flat)
    return o.reshape(batch, dil, sub, A_OUT), lse.reshape(batch, dil, sub, LANES)


def _compress_kernel(x_ref, pe_ref, w1_ref, w2_ref, gain_ref, cos_ref, sin_ref, seg_ref, o_ref):
    kv = pl.program_id(0)
    n = o_ref.shape[0]
    top = jnp.zeros((n, w1_ref.shape[-1]), F32)
    bot = jnp.zeros((n, w1_ref.shape[-1]), F32)
    for p in range(CMP_STRIDE):
        xp = x_ref[pl.ds(p, n, stride=CMP_STRIDE), :]
        top = top + _dot((xp + pe_ref[p:p + 1, :]).astype(BF16), w1_ref[p])
        bot = bot + _dot((xp + pe_ref[CMP_STRIDE + p:CMP_STRIDE + p + 1, :]).astype(BF16), w1_ref[CMP_STRIDE + p])
    hid = jax.nn.gelu(top + pltpu.roll(bot, n - 1, 0))
    out = _dot(hid.astype(BF16), w2_ref[...])

    @pl.when(kv == 0)
    def _():
        o_ref[...] = _head_norm_rope(out, gain_ref[...], cos_ref[...], sin_ref[...], seg_ref[...]).astype(o_ref.dtype)

    @pl.when(kv != 0)
    def _():
        o_ref[...] = out.astype(o_ref.dtype)


def _compress(fg, raw_block, pe, w1e, w2e, gain, cos_c, sin_c, seg, batch, seq):
    n_chunk = seq // CMP_STRIDE
    return pl.pallas_call(
        _compress_kernel,
        grid=(2, batch),
        in_specs=[
            pl.BlockSpec((seq, LANES), lambda kv, b: (b, raw_block + kv)),
            pl.BlockSpec((None,) + pe.shape[1:], lambda kv, b: (kv, 0, 0)),
            pl.BlockSpec((None,) + w1e.shape[1:], lambda kv, b: (kv, 0, 0, 0)),
            pl.BlockSpec((None,) + w2e.shape[1:], lambda kv, b: (kv, 0, 0)),
            pl.BlockSpec((1, LANES), lambda kv, b: (0, 0)),
            pl.BlockSpec((n_chunk, LANES), lambda kv, b: (0, 0)),
            pl.BlockSpec((n_chunk, LANES), lambda kv, b: (0, 0)),
            pl.BlockSpec((LANES, LANES), lambda kv, b: (0, 0)),
        ],
        out_specs=pl.BlockSpec((None, n_chunk, LANES), lambda kv, b: (kv, b, 0)),
        out_shape=jax.ShapeDtypeStruct((2, batch * n_chunk, LANES), BF16),
        compiler_params=_params("parallel", "parallel"),
        name="compress",
    )(fg, pe, w1e, w2e, gain, cos_c, sin_c, seg)


def _nsa_kernel(q_ref, ke_ref, vsel_ref, kwin_ref, vwin_ref, kc_ref, vc_ref, gate_ref, wslct_ref, o_ref,
                qa0, qa1, s0, s1, p0, p1, st0, st1, acc0, acc1, ot0, ot1, bias_ref, cmask_ref, imp_ref,
                *, n_cmp):
    qa, s_scr, p_scr, st = (qa0, qa1), (s0, s1), (p0, p1), (st0, st1)
    acc, ot = (acc0, acc1), (ot0, ot1)
    m_row, l_row, a_row = 0, 1, 2
    qi = pl.program_id(1)
    q0 = qi * TQ
    rows = B_GQA * TQ
    n_ck = kc_ref.shape[0]
    span = WIN + TQ
    n_full = q0 // SEL_TK
    w0 = pl.multiple_of(jnp.maximum(q0 - WIN, 0), TQ)
    lane = lax.broadcasted_iota(I32, (TQ, LANES), 1)
    low = lane < HEAD_DIM
    gates = jax.nn.sigmoid(gate_ref[...])

    def token_key(width):
        return (q0 + lax.broadcasted_iota(I32, (TQ, width), 0), lax.broadcasted_iota(I32, (TQ, width), 1))

    tr, cc = token_key(n_ck)
    cmp_valid = (cc * CMP_STRIDE + (CMP_LEN - 1) <= tr) & (cc < n_cmp)
    bias_ref[0, :, :n_ck] = jnp.where(cmp_valid, 0.0, NEG)
    cmask_ref[...] = jnp.where(cmp_valid, 1.0, 0.0)
    tr, cc = token_key(SEL_TK)
    bias_ref[1, :, :SEL_TK] = jnp.where(n_full * SEL_TK + cc <= tr, 0.0, NEG)
    tr, cc = token_key(span)
    dist = tr - (w0 + cc)
    bias_ref[2, :, :span] = jnp.where((dist >= 0) & (dist < WIN), 0.0, NEG)

    blk = lax.broadcasted_iota(I32, (LANES, TQ), 0)
    tok = q0 + lax.broadcasted_iota(I32, (LANES, TQ), 1)
    cur = jnp.right_shift(tok, SEL_SHIFT)
    forced = (blk == 0) | (blk == cur) | (blk == cur - 1)
    causal_blk = blk * SEL_LEN <= tok
    blk_f = blk.astype(F32)

    def gate_col(branch, h):
        base = branch * B_Q_HEADS + h * B_GQA
        return jnp.concatenate([gates[:, base + g:base + g + 1] for g in range(B_GQA)], axis=0)

    def lanes(c):
        return slice(c * LANES, (c + 1) * LANES)

    def score_cols(h, r0, width, mask):
        ts = r0 % TQ
        cols = [s_scr[h][r0:r0 + CH, lanes(c)] for c in range(width // LANES)]
        if mask is not None:
            cols = [x + bias_ref[mask, ts:ts + CH, lanes(c)] for c, x in enumerate(cols)]
        return cols

    def row_max(cols):
        mx = cols[0]
        for x in cols[1:]:
            mx = jnp.maximum(mx, x)
        return jnp.max(mx, axis=-1, keepdims=True)

    def row_sum(cols):
        tot = cols[0]
        for x in cols[1:]:
            tot = tot + x
        return jnp.sum(tot, axis=-1, keepdims=True)

    def softmax_cmp(h):
        for r0 in range(0, rows, CH):
            ts = r0 % TQ
            cols = score_cols(h, r0, n_ck, 0)
            m = row_max(cols)
            es = [jnp.exp2(x - m) * cmask_ref[ts:ts + CH, lanes(c)] for c, x in enumerate(cols)]
            rinv = 1.0 / jnp.maximum(row_sum(es), TINY)
            for c, e in enumerate(es):
                pn = e * rinv
                p_scr[h][r0:r0 + CH, lanes(c)] = pn.astype(BF16)
                if r0 < TQ:
                    imp_ref[h, ts:ts + CH, lanes(c)] = pn
                else:
                    imp_ref[h, ts:ts + CH, lanes(c)] += pn

    def softmax_online(h, mask):
        for r0 in range(0, rows, CH):
            rs = slice(r0, r0 + CH)
            cols = score_cols(h, r0, SEL_TK, mask)
            m_old = st[h][m_row, rs, :]
            m_new = jnp.maximum(m_old, row_max(cols))
            alpha = jnp.exp2(m_old - m_new)
            ps = [jnp.exp2(x - m_new) for x in cols]
            for c, pc in enumerate(ps):
                p_scr[h][rs, lanes(c)] = pc.astype(BF16)
            st[h][l_row, rs, :] = alpha * st[h][l_row, rs, :] + row_sum(ps)
            st[h][m_row, rs, :] = m_new
            st[h][a_row, rs, :] = alpha

    def softmax_once(h, width, mask):
        for r0 in range(0, rows, CH):
            rs = slice(r0, r0 + CH)
            cols = score_cols(h, r0, width, mask)
            m = row_max(cols)
            ps = [jnp.exp2(x - m) for x in cols]
            for c, pc in enumerate(ps):
                p_scr[h][rs, lanes(c)] = pc.astype(BF16)
            st[h][l_row, rs, :] = jnp.broadcast_to(row_sum(ps), (CH, LANES))

    def score_tile(h, kt):
        k0 = pl.multiple_of(kt * SEL_TK, SEL_TK)
        s_scr[h][:, :SEL_TK] = _dot_nt(qa[h][...], ke_ref[pl.ds(k0, SEL_TK), :])

    def value_tile(h, kt, live=None):
        k0 = pl.multiple_of(kt * SEL_TK, SEL_TK)
        pv = _dot(p_scr[h][:, :SEL_TK], vsel_ref[pl.ds(k0, SEL_TK), :])
        if live is not None:
            pv = jnp.where(live, pv, 0.0)
        acc[h][...] = st[h][a_row] * acc[h][...] + pv

    first = (lane & (HEAD_DIM // 2)) == 0
    for h in range(B_KV_HEADS):
        for g in range(B_GQA):
            c, par = (h * B_GQA + g) // 2, g % 2
            chunk = q_ref[:, lanes(c)].astype(F32)
            if par != h:
                chunk = pltpu.roll(chunk, HEAD_DIM // 2 if par == 0 else LANES - HEAD_DIM // 2, 1)
            chunk = jnp.where(first if h == 0 else jnp.logical_not(first), chunk, 0.0)
            qa[h][g * TQ:(g + 1) * TQ, :LANES] = chunk.astype(BF16)
        s_scr[h][:, :n_ck] = _dot_nt(qa[h][:, :LANES], kc_ref[...])

    for h in range(B_KV_HEADS):
        softmax_cmp(h)
        ot[h][...] = gate_col(0, h) * _dot(p_scr[h][:, :n_ck], vc_ref[...])

        p_slc = _dot_nt(wslct_ref[...], _split(imp_ref[h]))
        score = jnp.where(forced, FORCE_SCORE, jnp.where(causal_blk, p_slc, -1.0))
        for _ in range(N_SELECT):
            best = jnp.max(score, axis=0, keepdims=True)
            first = jnp.min(jnp.where(score == best, blk_f, float(LANES)), axis=0, keepdims=True)
            score = jnp.where(blk_f == first, -2.0, score)
        bias = jnp.where(score == -2.0, 0.0, NEG).T.astype(BF16)
        qa[h][:, LANES:] = jnp.concatenate([bias] * B_GQA, axis=0)

        st[h][m_row] = jnp.full((rows, LANES), NEG, F32)
        st[h][l_row] = jnp.zeros((rows, LANES), F32)
        st[h][a_row] = jnp.ones((rows, LANES), F32)
        acc[h][...] = jnp.zeros((rows, LANES), F32)

    score_tile(0, 0)

    def sweep(kt, carry):
        value_tile(1, jnp.maximum(kt - 1, 0), live=kt > 0)
        score_tile(1, kt)
        softmax_online(0, None)
        value_tile(0, kt)
        score_tile(0, kt + 1)
        softmax_online(1, None)
        return carry

    lax.fori_loop(0, n_full, sweep, 0)
    value_tile(1, jnp.maximum(n_full - 1, 0), live=n_full > 0)
    score_tile(1, n_full)
    softmax_online(0, 1)
    value_tile(0, n_full)
    softmax_online(1, 1)
    value_tile(1, n_full)

    kw = kwin_ref[pl.ds(w0, span), :]
    vw = vwin_ref[pl.ds(w0, span), :]
    for h in range(B_KV_HEADS):
        acc[h][...] = acc[h][...] / st[h][l_row]
        s_scr[h][:, :span] = _dot_nt(qa[h][:, :LANES], kw)
    for h in range(B_KV_HEADS):
        softmax_once(h, span, 2)
        o_win = _dot(p_scr[h][:, :span], vw) / st[h][l_row]
        o_tot = ot[h][...] + gate_col(1, h) * acc[h][...] + gate_col(2, h) * o_win

        for c2 in range(B_GQA // 2):
            halves = []
            for half in range(2):
                g = 2 * c2 + half
                piece = o_tot[g * TQ:(g + 1) * TQ]
                halves.append(piece if half == h else pltpu.roll(piece, HEAD_DIM, 1))
            o_ref[:, lanes(h * (B_GQA // 2) + c2)] = jnp.where(low, halves[0], halves[1]).astype(o_ref.dtype)


def _nsa(bqk, ke, bv, cmp_kv, fgate, wslct2, batch, seq, gate_block):
    t = bqk.shape[0]
    nq = seq // TQ
    n_ck = cmp_kv.shape[1] // batch
    n_cmp = seq // CMP_STRIDE - CMP_LEN // CMP_STRIDE + 1
    rows = B_GQA * TQ
    width = max(WIN + TQ, SEL_TK, n_ck)
    return pl.pallas_call(
        functools.partial(_nsa_kernel, n_cmp=n_cmp),
        grid=(batch, nq),
        in_specs=[
            pl.BlockSpec((TQ, B_QD), lambda b, i: (b * nq + i, 0)),
            pl.BlockSpec((seq, 2 * LANES), lambda b, i: (b, 0)),
            pl.BlockSpec((seq, LANES), lambda b, i: (b, 0)),
            pl.BlockSpec((seq, LANES), lambda b, i: (b, B_QD // LANES + 1)),
            pl.BlockSpec((seq, LANES), lambda b, i: (b, 1)),
            pl.BlockSpec((None, n_ck, LANES), lambda b, i: (0, b, 0)),
            pl.BlockSpec((None, n_ck, LANES), lambda b, i: (1, b, 0)),
            pl.BlockSpec((TQ, LANES), lambda b, i: (b * nq + i, gate_block)),
            pl.BlockSpec(wslct2.shape, lambda b, i: (0, 0)),
        ],
        out_specs=pl.BlockSpec((TQ, B_OUT), lambda b, i: (b * nq + i, 0)),
        out_shape=jax.ShapeDtypeStruct((t, B_OUT), BF16),
        scratch_shapes=(
            [pltpu.VMEM((rows, 2 * LANES), BF16)] * 2 +
            [pltpu.VMEM((rows, width), F32)] * 2 +
            [pltpu.VMEM((rows, width), BF16)] * 2 +
            [pltpu.VMEM((3, rows, LANES), F32)] * 2 +
            [pltpu.VMEM((rows, LANES), F32)] * 4 +
            [pltpu.VMEM((3, TQ, width), F32), pltpu.VMEM((TQ, n_ck), F32), pltpu.VMEM((2, TQ, n_ck), F32)]),
        compiler_params=_params("parallel", "parallel"),
        name="nsa_mixer",
    )(bqk, ke, bv, bqk, bv, cmp_kv, cmp_kv, fgate, wslct2)


def _merge_kernel(o0_ref, o1_ref, o2_ref, l0_ref, l1_ref, l2_ref, ob_ref, m0_ref, m1_ref, x_ref,
                  wpa_ref, wpb_ref, wout_ref, exp_ref, g_ref, xo_ref, h_ref, osc_ref, lsc_ref, oa_ref, mix_ref):
    def token_order(ref, scr, slot):
        d, n, width = ref.shape
        if d == 1:
            return ref[0]
        chunks = width // LANES
        for c in range(chunks):
            for r in range(d):
                scr[slot * chunks + c, pl.ds(r, n, stride=d), :] = ref[r, :, c * LANES:(c + 1) * LANES]
        return jnp.concatenate([scr[slot * chunks + c] for c in range(chunks)], axis=1)

    l0, l1, l2 = token_order(l0_ref, lsc_ref, 0), token_order(l1_ref, lsc_ref, 0), token_order(l2_ref, lsc_ref, 1)
    o0, o1, o2 = token_order(o0_ref, osc_ref, 0), token_order(o1_ref, osc_ref, 0), token_order(o2_ref, osc_ref, 1)
    mx = jnp.maximum(jnp.maximum(l0, l1), l2)
    e0, e1, e2 = jnp.exp2(l0 - mx), jnp.exp2(l1 - mx), jnp.exp2(l2 - mx)
    den = e0 + e1 + e2
    ex = exp_ref[...]
    out_a = _split_dot(e0 / den, ex) * o0 + _split_dot(e1 / den, ex) * o1 + _split_dot(e2 / den, ex) * o2
    oa_ref[...] = out_a.astype(BF16)
    d = x_ref.shape[1]
    for c0 in range(0, d, MXU_COLS):
        sl = slice(c0, c0 + MXU_COLS)
        pa = _dot(oa_ref[...], wpa_ref[:, sl])
        pb = _dot(ob_ref[...], wpb_ref[:, sl])
        mixed = jax.nn.sigmoid(m0_ref[:, sl]) * pa + jax.nn.sigmoid(m1_ref[:, sl]) * pb
        mix_ref[:, sl] = mixed.astype(BF16)
    ssq = None
    for c0 in range(0, d, MXU_COLS):
        sl = slice(c0, c0 + MXU_COLS)
        x = x_ref[:, sl] + _dot(mix_ref[...], wout_ref[:, sl])
        xo_ref[:, sl] = x
        part = jnp.sum(x * x, axis=-1, keepdims=True)
        ssq = part if ssq is None else ssq + part
    scale = lax.rsqrt(ssq * (1.0 / d) + EPS)
    for c0 in range(0, d, MXU_COLS):
        sl = slice(c0, c0 + MXU_COLS)
        h_ref[:, sl] = (xo_ref[:, sl] * scale * g_ref[:, sl]).astype(h_ref.dtype)


def _merge(o_groups, lse_groups, out_b, fgate, x, wpa, wpb, wout, expand2, g_next, seq, tm=512):
    t, d = x.shape
    nseq = seq // tm
    row = lambda w: pl.BlockSpec((tm, w), lambda i: (i, 0))
    full = lambda a: pl.BlockSpec(a.shape, lambda i: (0,) * a.ndim)
    grouped = lambda a: pl.BlockSpec((None, a.shape[1], tm // a.shape[1], a.shape[3]),
                                     lambda i: (i // nseq, 0, i % nseq, 0))
    g_next = g_next.reshape(1, d)
    return pl.pallas_call(
        _merge_kernel,
        grid=(t // tm,),
        in_specs=[grouped(a) for a in o_groups] + [grouped(a) for a in lse_groups] + [
            row(B_OUT), pl.BlockSpec((tm, d), lambda i: (i, 0)), pl.BlockSpec((tm, d), lambda i: (i, 1)), row(d),
            full(wpa), full(wpb), full(wout), full(expand2), full(g_next)],
        out_specs=[row(d), row(d)],
        out_shape=[jax.ShapeDtypeStruct((t, d), F32), jax.ShapeDtypeStruct((t, d), BF16)],
        scratch_shapes=[pltpu.VMEM((2 * A_OUT // LANES, tm, LANES), F32), pltpu.VMEM((2, tm, LANES), F32),
                        pltpu.VMEM((tm, A_OUT), BF16), pltpu.VMEM((tm, d), BF16)],
        compiler_params=_params("parallel"),
        name="merge_out",
    )(*o_groups, *lse_groups, out_b, fgate, fgate, x, wpa, wpb, wout, expand2, g_next)


def _ffn_up_kernel(h_ref, wg_ref, wu_ref, cw_ref, cb_ref, o_ref, gbuf_ref, *, tiles_per_seq):
    i = pl.program_id(1)
    tm = h_ref.shape[0]
    h = h_ref[...]

    @pl.when(i % tiles_per_seq == 0)
    def _():
        gbuf_ref[0:8, :] = jnp.zeros((8, gbuf_ref.shape[1]), F32)

    @pl.when(i % tiles_per_seq != 0)
    def _():
        gbuf_ref[0:8, :] = gbuf_ref[tm:tm + 8, :]

    gbuf_ref[8:tm + 8, :] = _dot(h, wg_ref[...])
    u = _dot(h, wu_ref[...])
    conv = cb_ref[...] + cw_ref[0:1, :] * gbuf_ref[pl.ds(6, tm), :]
    conv = conv + cw_ref[1:2, :] * gbuf_ref[pl.ds(7, tm), :]
    conv = conv + cw_ref[2:3, :] * gbuf_ref[pl.ds(8, tm), :]
    o_ref[...] = (jax.nn.silu(conv) * u).astype(o_ref.dtype)


def _ffn_up(h, w_up, conv_w, conv_b, seq, tm=512, tn=1408):
    t, d = h.shape
    nj = D_FF // tn
    return pl.pallas_call(
        functools.partial(_ffn_up_kernel, tiles_per_seq=seq // tm),
        grid=(nj, t // tm),
        in_specs=[
            pl.BlockSpec((tm, d), lambda j, i: (i, 0)),
            pl.BlockSpec((d, tn), lambda j, i: (0, j)),
            pl.BlockSpec((d, tn), lambda j, i: (0, nj + j)),
            pl.BlockSpec((CONV_W, tn), lambda j, i: (0, j)),
            pl.BlockSpec((1, tn), lambda j, i: (0, j)),
        ],
        out_specs=pl.BlockSpec((tm, tn), lambda j, i: (i, j)),
        out_shape=jax.ShapeDtypeStruct((t, D_FF), BF16),
        scratch_shapes=[pltpu.VMEM((tm + 8, tn), F32)],
        compiler_params=_params("arbitrary", "arbitrary"),
        name="ffn_up_conv",
    )(h, w_up, w_up, conv_w, conv_b.reshape(1, D_FF))


def _ffn_down_kernel(a_ref, w_ref, x_ref, g_ref, xo_ref, h_ref):
    x = x_ref[...] + _dot(a_ref[...], w_ref[...])
    xo_ref[...] = x
    y = x * lax.rsqrt(jnp.mean(x * x, axis=-1, keepdims=True) + EPS)
    h_ref[...] = (y * g_ref[...]).astype(h_ref.dtype)


def _ffn_down(act, w_down, x, g_next, tm=512):
    t, d = x.shape
    row = lambda w: pl.BlockSpec((tm, w), lambda i: (i, 0))
    return pl.pallas_call(
        _ffn_down_kernel,
        grid=(t // tm,),
        in_specs=[row(D_FF), pl.BlockSpec(w_down.shape, lambda i: (0, 0)), row(d),
                  pl.BlockSpec((1, d), lambda i: (0, 0))],
        out_specs=[row(d), row(d)],
        out_shape=[jax.ShapeDtypeStruct((t, d), F32), jax.ShapeDtypeStruct((t, d), BF16)],
        compiler_params=_params("parallel"),
        name="ffn_down",
    )(act, w_down, x, g_next.reshape(1, d))


def _rope_tables(pos):
    half = HEAD_DIM // 2
    inv_freq = ROPE_THETA ** (-jnp.arange(half, dtype=F32) / half)
    ang = pos.astype(F32)[:, None] * inv_freq[None, :]
    c, s = jnp.cos(ang), jnp.sin(ang)
    return jnp.tile(c, (1, 4)), jnp.concatenate([-s, -s, s, s], axis=1)


def _pair_order(n):
    half = HEAD_DIM // 2
    pair = np.concatenate([np.arange(0, half), np.arange(HEAD_DIM, HEAD_DIM + half),
                           np.arange(half, HEAD_DIM), np.arange(HEAD_DIM + half, LANES)])
    cols = np.arange(n)
    return (cols // LANES) * LANES + pair[cols % LANES]


def _pair_gain(g):
    half = HEAD_DIM // 2
    return jnp.concatenate([g[:half], g[:half], g[half:], g[half:]])


def _constants(n_ck, seq):
    lane = np.arange(LANES)
    head = (lane // (HEAD_DIM // 2)) % 2
    seg = (head[:, None] == head[None, :]).astype(np.float32) / HEAD_DIM
    ratio = SEL_LEN // CMP_STRIDE
    wslc = np.zeros((n_ck, LANES), np.float32)
    for off in range(CMP_LEN // CMP_STRIDE):
        for i in range(n_ck):
            if i + off < n_ck:
                wslc[i, (i + off) // ratio] += 1.0
    wslct2 = np.concatenate([wslc, wslc], axis=0).T
    expand = np.zeros((LANES, A_OUT), np.float32)
    for j in range(A_HEADS_PER_GROUP):
        expand[j, j * HEAD_DIM:(j + 1) * HEAD_DIM] = 1.0
    expand2 = np.concatenate([expand, expand], axis=0)
    onehot = (np.arange(seq)[:, None] // SEL_LEN == lane[None, :]).astype(np.float32)
    return (jnp.asarray(seg, BF16), jnp.asarray(wslct2, BF16), jnp.asarray(expand2, BF16),
            jnp.asarray(onehot, BF16))


def kernel(x, norm_mix_g, w_in, a_q_g, a_k_g, b_q_g, b_k_g, cmp_pos, cmp_w1, cmp_w2, w_proj_a, w_proj_b,
           w_out, norm_ffn_g, w_up, conv_w, conv_b, w_down):
    batch, seq, d = x.shape
    depth = w_in.shape[0]
    t = batch * seq
    assert d == D_MODEL and seq % (16 * TQ) == 0 and seq // SEL_LEN <= LANES
    n_ck = seq // CMP_STRIDE
    seg, wslct2, expand2, onehot = _constants(n_ck, seq)
    onehot = jnp.tile(onehot, (batch, 1))
    cos_t, sin_t = _rope_tables(jnp.arange(seq))
    cos_c, sin_c = _rope_tables(jnp.arange(n_ck) * CMP_STRIDE + CMP_LEN - 1)
    eye_h = jnp.eye(B_KV_HEADS, dtype=F32)
    o_kq, o_kv, o_gate, o_merge = A_QKV, A_QKV + B_QD, A_QKV + B_QD + B_KV, A_QKV + B_QD + B_KV + B_GATE
    n_a = A_HEADS * HEAD_DIM
    f_gate_block = 2 * D_MODEL // LANES

    xf = x.reshape(t, d)
    h = _rmsnorm(xf, norm_mix_g[0])
    for l in range(depth):
        w = w_in[l]
        kv = lambda i: w[:, o_kv + i * LANES:o_kv + (i + 1) * LANES]
        w_bqk = jnp.concatenate([w[:, o_kq:o_kv], kv(2), kv(4)], axis=1)[:, _pair_order(B_QD + 2 * LANES)]
        w_bqk = w_bqk.astype(BF16)
        w_bv = jnp.concatenate([kv(3), kv(5)], axis=1).astype(BF16)
        w_f = jnp.concatenate([w[:, o_merge:], w[:, o_gate:o_merge],
                               jnp.zeros((d, LANES - B_GATE), F32), kv(0), kv(1)], axis=1).astype(BF16)
        gain_a = jnp.concatenate([jnp.tile(_pair_gain(a_q_g[l] * Q_SCALE), A_HEADS_PER_GROUP // 2),
                                  jnp.tile(_pair_gain(a_k_g[l]), A_HEADS_PER_GROUP // 2)]).reshape(1, -1)
        gain_b = jnp.concatenate([jnp.tile(_pair_gain(b_q_g[l] * Q_SCALE), B_Q_HEADS // 2),
                                  _pair_gain(b_k_g[l, 1]), _pair_gain(b_k_g[l, 2])]).reshape(1, -1)

        o_groups, lse_groups = [], []
        for gi, (_, dil) in enumerate(DIL_PAIRS):
            cols = [w[:, part * n_a + gi * A_OUT:part * n_a + (gi + 1) * A_OUT] for part in range(3)]
            cols = [cols[0][:, _pair_order(A_OUT)], cols[1][:, _pair_order(A_OUT)], cols[2]]
            qkv = _proj_group(h, jnp.concatenate(cols, axis=1).astype(BF16), gain_a, cos_t, sin_t, seg,
                              dil, batch, seq)
            o_g, lse_g = _band_attention(qkv, dil, batch, seq)
            o_groups.append(o_g)
            lse_groups.append(lse_g)

        bqk = _proj(h, w_bqk, BF16, tn=768, rope=(gain_b, cos_t, sin_t, seg), seq=seq)
        bv = _proj(h, w_bv, BF16, tn=256)
        fg = _proj(h, w_f, F32, tn=w_f.shape[1], tm=256)
        pe = jnp.tile(cmp_pos[l], (1, 1, B_KV_HEADS))
        w1 = cmp_w1[l].reshape(2, CMP_LEN, HEAD_DIM, CMP_HIDDEN)
        w1e = jnp.einsum('kpdn,hj->kphdjn', w1, eye_h).reshape(2, CMP_LEN, LANES, 2 * CMP_HIDDEN)
        w2e = jnp.einsum('knd,hj->khnjd', cmp_w2[l], eye_h).reshape(2, 2 * CMP_HIDDEN, LANES)
        w2e = jnp.stack([w2e[0][:, _pair_order(LANES)], w2e[1]])
        gain_c = _pair_gain(b_k_g[l, 0]).reshape(1, LANES)
        cmp_kv = _compress(fg, f_gate_block + 1, pe, w1e.astype(BF16), w2e.astype(BF16), gain_c,
                           cos_c, sin_c, seg, batch, seq)
        ke = jnp.concatenate([bqk[:, B_QD:B_QD + LANES], onehot], axis=1)
        out_b = _nsa(bqk, ke, bv, cmp_kv, fg, wslct2, batch, seq, f_gate_block)

        xf, h = _merge(o_groups, lse_groups, out_b, fg, xf, w_proj_a[l].astype(BF16), w_proj_b[l].astype(BF16),
                       w_out[l].astype(BF16), expand2, norm_ffn_g[l], seq)
        act = _ffn_up(h, w_up[l].astype(BF16), conv_w[l], conv_b[l], seq)
        g_next = norm_mix_g[l + 1] if l + 1 < depth else jnp.ones((d,), F32)
        xf, h = _ffn_down(act, w_down[l].astype(BF16), xf, g_next)
    return xf.reshape(batch, seq, d)
```

```python
import functools

import numpy as np
import jax
import jax.numpy as jnp
from jax import lax
from jax.experimental import pallas as pl
from jax.experimental.pallas import tpu as pltpu

F32 = jnp.float32
BF16 = jnp.bfloat16
I32 = jnp.int32

D_MODEL = 1024
HEAD_DIM = 64
ROPE_THETA = 10000.0
EPS = 1e-6
NEG = -1e30
TINY = 1e-30
FORCE_SCORE = 1e9

DIL_PAIRS = ((128, 1), (512, 4), (2048, 16))
A_HEADS_PER_GROUP = 8
A_HEADS = 24
A_OUT = 512
A_BAND = 128

B_Q_HEADS = 8
B_KV_HEADS = 2
B_GQA = 4
CMP_LEN = 32
CMP_STRIDE = 16
CMP_HIDDEN = 256
SEL_LEN = 64
SEL_SHIFT = 6
N_SELECT = 16
WIN = 512
B_OUT = 512

D_FF = 2816
CONV_W = 3

A_QKV = 3 * A_HEADS * HEAD_DIM
B_QD = B_Q_HEADS * HEAD_DIM
B_KV = 3 * 2 * B_KV_HEADS * HEAD_DIM
B_GATE = 3 * B_Q_HEADS

LOG2E = 1.4426950408889634
Q_SCALE = LOG2E * HEAD_DIM ** -0.5

LANES = 128
MXU_COLS = 256
TQ = 128
BAND_BLOCKS = 4
SEL_TK = 512
CH = 64
VMEM_LIMIT = 48 * 1024 * 1024

_NT = (((1,), (1,)), ((), ()))


def _dot(a, b):
    return jnp.dot(a, b, preferred_element_type=F32)


def _dot_nt(a, b):
    return lax.dot_general(a, b, _NT, preferred_element_type=F32)


def _split(x):
    hi = x.astype(BF16)
    lo = (x - hi.astype(F32)).astype(BF16)
    return jnp.concatenate([hi, lo], axis=1)


def _split_dot(x, w2):
    return _dot(_split(x), w2)


def _params(*sem):
    return pltpu.CompilerParams(dimension_semantics=sem, vmem_limit_bytes=VMEM_LIMIT)


def _rmsnorm_kernel(x_ref, g_ref, h_ref):
    x = x_ref[...]
    y = x * lax.rsqrt(jnp.mean(x * x, axis=-1, keepdims=True) + EPS)
    h_ref[...] = (y * g_ref[...]).astype(h_ref.dtype)


def _rmsnorm(x, g, tm=512):
    t, d = x.shape
    return pl.pallas_call(
        _rmsnorm_kernel,
        grid=(t // tm,),
        in_specs=[pl.BlockSpec((tm, d), lambda i: (i, 0)), pl.BlockSpec((1, d), lambda i: (0, 0))],
        out_specs=pl.BlockSpec((tm, d), lambda i: (i, 0)),
        out_shape=jax.ShapeDtypeStruct((t, d), BF16),
        compiler_params=_params("parallel"),
        name="rmsnorm",
    )(x, g.reshape(1, d))


def _head_norm_rope(y, gain, cos_t, sin_t, seg):
    ms = _dot((y * y).astype(BF16), seg)
    yn = y * lax.rsqrt(ms + EPS) * gain
    return yn * cos_t + pltpu.roll(yn, HEAD_DIM, 1) * sin_t


def _proj_kernel(h_ref, w_ref, o_ref):
    o_ref[...] = _dot(h_ref[...], w_ref[...]).astype(o_ref.dtype)


def _proj_rope_kernel(h_ref, w_ref, gain_ref, cos_ref, sin_ref, seg_ref, o_ref):
    y = _dot(h_ref[...], w_ref[...])
    cos_t, sin_t, seg = cos_ref[...], sin_ref[...], seg_ref[...]
    for c in range(y.shape[1] // LANES):
        sl = slice(c * LANES, (c + 1) * LANES)
        o_ref[:, sl] = _head_norm_rope(y[:, sl], gain_ref[:, sl], cos_t, sin_t, seg).astype(o_ref.dtype)


def _proj(h, w, out_dtype, tn, tm=512, rope=None, seq=None):
    t, k = h.shape
    n = w.shape[1]
    grid = (n // tn, t // tm)
    h_spec = pl.BlockSpec((tm, k), lambda j, i: (i, 0))
    w_spec = pl.BlockSpec((k, tn), lambda j, i: (0, j))
    o_spec = pl.BlockSpec((tm, tn), lambda j, i: (i, j))
    out_shape = jax.ShapeDtypeStruct((t, n), out_dtype)
    if rope is None:
        return pl.pallas_call(_proj_kernel, grid=grid, in_specs=[h_spec, w_spec], out_specs=o_spec,
                              out_shape=out_shape, compiler_params=_params("parallel", "parallel"),
                              name="proj")(h, w)
    gain, cos_t, sin_t, seg = rope
    nseq = seq // tm
    return pl.pallas_call(
        _proj_rope_kernel, grid=grid,
        in_specs=[h_spec, w_spec,
                  pl.BlockSpec((1, tn), lambda j, i: (0, j)),
                  pl.BlockSpec((tm, LANES), lambda j, i: (i % nseq, 0)),
                  pl.BlockSpec((tm, LANES), lambda j, i: (i % nseq, 0)),
                  pl.BlockSpec((LANES, LANES), lambda j, i: (0, 0))],
        out_specs=o_spec, out_shape=out_shape,
        compiler_params=_params("parallel", "parallel"), name="proj_rope",
    )(h, w, gain, cos_t, sin_t, seg)


def _proj_group_kernel(h_ref, w_ref, gain_ref, cos_ref, sin_ref, seg_ref, o_ref, y_ref, *, n_rope):
    dil, n_sub, _ = o_ref.shape
    y = _dot(h_ref[...], w_ref[...])
    cos_t, sin_t, seg = cos_ref[...], sin_ref[...], seg_ref[...]
    for c in range(y.shape[1] // LANES):
        sl = slice(c * LANES, (c + 1) * LANES)
        val = _head_norm_rope(y[:, sl], gain_ref[:, sl], cos_t, sin_t, seg) if c < n_rope else y[:, sl]
        if dil == 1:
            o_ref[0, :, sl] = val.astype(o_ref.dtype)
        else:
            y_ref[c] = val
            for r in range(dil):
                o_ref[r, :, sl] = y_ref[c, pl.ds(r, n_sub, stride=dil), :].astype(o_ref.dtype)


def _proj_group(h, w, gain, cos_t, sin_t, seg, dil, batch, seq, tm=512):
    t, k = h.shape
    n = w.shape[1]
    nseq = seq // tm
    return pl.pallas_call(
        functools.partial(_proj_group_kernel, n_rope=gain.shape[1] // LANES),
        grid=(t // tm,),
        in_specs=[pl.BlockSpec((tm, k), lambda i: (i, 0)),
                  pl.BlockSpec((k, n), lambda i: (0, 0)),
                  pl.BlockSpec(gain.shape, lambda i: (0, 0)),
                  pl.BlockSpec((tm, LANES), lambda i: (i % nseq, 0)),
                  pl.BlockSpec((tm, LANES), lambda i: (i % nseq, 0)),
                  pl.BlockSpec((LANES, LANES), lambda i: (0, 0))],
        out_specs=pl.BlockSpec((None, dil, tm // dil, n), lambda i: (i // nseq, 0, i % nseq, 0)),
        out_shape=jax.ShapeDtypeStruct((batch, dil, seq // dil, n), BF16),
        scratch_shapes=[pltpu.VMEM((n // LANES, tm, LANES), F32)],
        compiler_params=_params("parallel"),
        name=f"proj_group_d{dil}",
    )(h, w, gain, cos_t, sin_t, seg)


def _band_kernel(q_ref, kp_ref, kc_ref, vp_ref, vc_ref, o_ref, lse_ref,
                 kk_ref, vv_ref, s_ref, p_ref, st_ref, bias_ref):
    i = pl.program_id(1)
    n_blk = q_ref.shape[0] // TQ
    n_pair = A_HEADS_PER_GROUP // 2
    kk_ref[0:TQ] = kp_ref[...]
    kk_ref[TQ:] = kc_ref[...]
    vv_ref[0:TQ] = vp_ref[...]
    vv_ref[TQ:] = vc_ref[...]
    row = lax.broadcasted_iota(I32, (TQ, 2 * TQ), 0)
    col = lax.broadcasted_iota(I32, (TQ, 2 * TQ), 1)
    dist = TQ + row - col
    band = (dist >= 0) & (dist <= A_BAND)
    bias_ref[0] = jnp.where(band & ((col >= TQ) | (i > 0)), 0.0, NEG)
    bias_ref[1] = jnp.where(band, 0.0, NEG)
    lane = lax.broadcasted_iota(I32, (TQ, LANES), 1)
    low = lane < HEAD_DIM
    first = (lane & (HEAD_DIM // 2)) == 0

    def lanes(c):
        return slice(c * LANES, (c + 1) * LANES)

    def scores(j, p, slot):
        q2 = q_ref[j * TQ:(j + 1) * TQ, lanes(p)].astype(F32)
        qs = jnp.concatenate([jnp.where(first, q2, 0.0), jnp.where(first, 0.0, q2)], axis=0).astype(BF16)
        s_ref[slot] = _dot_nt(qs, kk_ref[j * TQ:(j + 2) * TQ, lanes(p)])

    def softmax(j, slot):
        for r0 in range(0, 2 * TQ, CH):
            rs, ts = slice(r0, r0 + CH), slice(r0 % TQ, r0 % TQ + CH)
            cols = [s_ref[slot, rs, lanes(c)] + bias_ref[min(j, 1), ts, lanes(c)] for c in range(2)]
            m = jnp.max(jnp.maximum(cols[0], cols[1]), axis=-1, keepdims=True)
            ps = [jnp.exp2(x - m) for x in cols]
            for c, pc in enumerate(ps):
                p_ref[slot, rs, lanes(c)] = pc.astype(BF16)
            st_ref[slot, 0, rs, :] = jnp.broadcast_to(m, (CH, LANES))
            st_ref[slot, 1, rs, :] = jnp.broadcast_to(jnp.sum(ps[0] + ps[1], axis=-1, keepdims=True), (CH, LANES))

    def values(j, p, slot, lse_tile):
        l = st_ref[slot, 1]
        o2 = _dot(p_ref[slot], vv_ref[j * TQ:(j + 2) * TQ, lanes(p)]) / l
        o_ref[j * TQ:(j + 1) * TQ, lanes(p)] = jnp.where(low, o2[:TQ], o2[TQ:])
        lse2 = st_ref[slot, 0] + jnp.log(l) * LOG2E
        return jnp.where(lane == 2 * p, lse2[:TQ], jnp.where(lane == 2 * p + 1, lse2[TQ:], lse_tile))

    units = [(j, p) for j in range(n_blk) for p in range(n_pair)]
    scores(*units[0], 0)
    lse_tile = jnp.zeros((TQ, LANES), F32)
    for u, (j, p) in enumerate(units):
        if u + 1 < len(units):
            scores(*units[u + 1], (u + 1) % 2)
        softmax(j, u % 2)
        lse_tile = values(j, p, u % 2, lse_tile)
        if p == n_pair - 1:
            lse_ref[j * TQ:(j + 1) * TQ, :] = lse_tile


def _band_attention(qkv, dil, batch, seq):
    sub = seq // dil
    nb = sub // TQ
    n_blk = min(BAND_BLOCKS, nb)
    rows = n_blk * TQ
    steps = nb // n_blk
    n_sub = batch * dil
    flat = qkv.reshape(n_sub * sub, qkv.shape[-1])

    def cur(n, i):
        return n * steps + i

    def prev(n, i):
        return n * nb + jnp.maximum(i * n_blk - 1, 0)

    o, lse = pl.pallas_call(
        _band_kernel,
        grid=(n_sub, steps),
        in_specs=[
            pl.BlockSpec((rows, A_OUT), lambda n, i: (cur(n, i), 0)),
            pl.BlockSpec((TQ, A_OUT), lambda n, i: (prev(n, i), 1)),
            pl.BlockSpec((rows, A_OUT), lambda n, i: (cur(n, i), 1)),
            pl.BlockSpec((TQ, A_OUT), lambda n, i: (prev(n, i), 2)),
            pl.BlockSpec((rows, A_OUT), lambda n, i: (cur(n, i), 2)),
        ],
        out_specs=[pl.BlockSpec((rows, A_OUT), lambda n, i: (cur(n, i), 0)),
                   pl.BlockSpec((rows, LANES), lambda n, i: (cur(n, i), 0))],
        out_shape=[jax.ShapeDtypeStruct((n_sub * sub, A_OUT), F32),
                   jax.ShapeDtypeStruct((n_sub * sub, LANES), F32)],
        scratch_shapes=[pltpu.VMEM((rows + TQ, A_OUT), BF16), pltpu.VMEM((rows + TQ, A_OUT), BF16),
                        pltpu.VMEM((2, 2 * TQ, 2 * TQ), F32), pltpu.VMEM((2, 2 * TQ, 2 * TQ), BF16),
                        pltpu.VMEM((2, 2, 2 * TQ, LANES), F32), pltpu.VMEM((2, TQ, 2 * TQ), F32)],
        compiler_params=_params("parallel", "parallel"),
        name=f"band_attention_d{dil}",
    )(flat, flat, flat, flat, flat)
    return o.reshape(batch, dil, sub, A_OUT), lse.reshape(batch, dil, sub, LANES)


def _compress_kernel(x_ref, pe_ref, w1_ref, w2_ref, gain_ref, cos_ref, sin_ref, seg_ref, o_ref):
    kv = pl.program_id(0)
    n = o_ref.shape[0]
    top = jnp.zeros((n, w1_ref.shape[-1]), F32)
    bot = jnp.zeros((n, w1_ref.shape[-1]), F32)
    for p in range(CMP_STRIDE):
        xp = x_ref[pl.ds(p, n, stride=CMP_STRIDE), :]
        top = top + _dot((xp + pe_ref[p:p + 1, :]).astype(BF16), w1_ref[p])
        bot = bot + _dot((xp + pe_ref[CMP_STRIDE + p:CMP_STRIDE + p + 1, :]).astype(BF16), w1_ref[CMP_STRIDE + p])
    hid = jax.nn.gelu(top + pltpu.roll(bot, n - 1, 0))
    out = _dot(hid.astype(BF16), w2_ref[...])

    @pl.when(kv == 0)
    def _():
        o_ref[...] = _head_norm_rope(out, gain_ref[...], cos_ref[...], sin_ref[...], seg_ref[...]).astype(o_ref.dtype)

    @pl.when(kv != 0)
    def _():
        o_ref[...] = out.astype(o_ref.dtype)


def _compress(fg, raw_block, pe, w1e, w2e, gain, cos_c, sin_c, seg, batch, seq):
    n_chunk = seq // CMP_STRIDE
    return pl.pallas_call(
        _compress_kernel,
        grid=(2, batch),
        in_specs=[
            pl.BlockSpec((seq, LANES), lambda kv, b: (b, raw_block + kv)),
            pl.BlockSpec((None,) + pe.shape[1:], lambda kv, b: (kv, 0, 0)),
            pl.BlockSpec((None,) + w1e.shape[1:], lambda kv, b: (kv, 0, 0, 0)),
            pl.BlockSpec((None,) + w2e.shape[1:], lambda kv, b: (kv, 0, 0)),
            pl.BlockSpec((1, LANES), lambda kv, b: (0, 0)),
            pl.BlockSpec((n_chunk, LANES), lambda kv, b: (0, 0)),
            pl.BlockSpec((n_chunk, LANES), lambda kv, b: (0, 0)),
            pl.BlockSpec((LANES, LANES), lambda kv, b: (0, 0)),
        ],
        out_specs=pl.BlockSpec((None, n_chunk, LANES), lambda kv, b: (kv, b, 0)),
        out_shape=jax.ShapeDtypeStruct((2, batch * n_chunk, LANES), BF16),
        compiler_params=_params("parallel", "parallel"),
        name="compress",
    )(fg, pe, w1e, w2e, gain, cos_c, sin_c, seg)


def _nsa_kernel(q_ref, ke_ref, vsel_ref, kwin_ref, vwin_ref, kc_ref, vc_ref, gate_ref, wslct_ref, o_ref,
                qa0, qa1, s0, s1, p0, p1, st0, st1, acc0, acc1, ot0, ot1, bias_ref, cmask_ref, imp_ref,
                *, n_cmp):
    qa, s_scr, p_scr, st = (qa0, qa1), (s0, s1), (p0, p1), (st0, st1)
    acc, ot = (acc0, acc1), (ot0, ot1)
    m_row, l_row, a_row = 0, 1, 2
    qi = pl.program_id(1)
    q0 = qi * TQ
    rows = B_GQA * TQ
    n_ck = kc_ref.shape[0]
    span = WIN + TQ
    n_full = q0 // SEL_TK
    w0 = pl.multiple_of(jnp.maximum(q0 - WIN, 0), TQ)
    lane = lax.broadcasted_iota(I32, (TQ, LANES), 1)
    low = lane < HEAD_DIM
    gates = jax.nn.sigmoid(gate_ref[...])

    def token_key(width):
        return (q0 + lax.broadcasted_iota(I32, (TQ, width), 0), lax.broadcasted_iota(I32, (TQ, width), 1))

    tr, cc = token_key(n_ck)
    cmp_valid = (cc * CMP_STRIDE + (CMP_LEN - 1) <= tr) & (cc < n_cmp)
    bias_ref[0, :, :n_ck] = jnp.where(cmp_valid, 0.0, NEG)
    cmask_ref[...] = jnp.where(cmp_valid, 1.0, 0.0)
    tr, cc = token_key(SEL_TK)
    bias_ref[1, :, :SEL_TK] = jnp.where(n_full * SEL_TK + cc <= tr, 0.0, NEG)
    tr, cc = token_key(span)
    dist = tr - (w0 + cc)
    bias_ref[2, :, :span] = jnp.where((dist >= 0) & (dist < WIN), 0.0, NEG)

    blk = lax.broadcasted_iota(I32, (LANES, TQ), 0)
    tok = q0 + lax.broadcasted_iota(I32, (LANES, TQ), 1)
    cur = jnp.right_shift(tok, SEL_SHIFT)
    forced = (blk == 0) | (blk == cur) | (blk == cur - 1)
    causal_blk = blk * SEL_LEN <= tok
    blk_f = blk.astype(F32)

    def gate_col(branch, h):
        base = branch * B_Q_HEADS + h * B_GQA
        return jnp.concatenate([gates[:, base + g:base + g + 1] for g in range(B_GQA)], axis=0)

    def lanes(c):
        return slice(c * LANES, (c + 1) * LANES)

    def score_cols(h, r0, width, mask):
        ts = r0 % TQ
        cols = [s_scr[h][r0:r0 + CH, lanes(c)] for c in range(width // LANES)]
        if mask is not None:
            cols = [x + bias_ref[mask, ts:ts + CH, lanes(c)] for c, x in enumerate(cols)]
        return cols

    def row_max(cols):
        mx = cols[0]
        for x in cols[1:]:
            mx = jnp.maximum(mx, x)
        return jnp.max(mx, axis=-1, keepdims=True)

    def row_sum(cols):
        tot = cols[0]
        for x in cols[1:]:
            tot = tot + x
        return jnp.sum(tot, axis=-1, keepdims=True)

    def softmax_cmp(h):
        for r0 in range(0, rows, CH):
            ts = r0 % TQ
            cols = score_cols(h, r0, n_ck, 0)
            m = row_max(cols)
            es = [jnp.exp2(x - m) * cmask_ref[ts:ts + CH, lanes(c)] for c, x in enumerate(cols)]
            rinv = 1.0 / jnp.maximum(row_sum(es), TINY)
            for c, e in enumerate(es):
                pn = e * rinv
                p_scr[h][r0:r0 + CH, lanes(c)] = pn.astype(BF16)
                if r0 < TQ:
                    imp_ref[h, ts:ts + CH, lanes(c)] = pn
                else:
                    imp_ref[h, ts:ts + CH, lanes(c)] += pn

    def softmax_online(h, mask):
        for r0 in range(0, rows, CH):
            rs = slice(r0, r0 + CH)
            cols = score_cols(h, r0, SEL_TK, mask)
            m_old = st[h][m_row, rs, :]
            m_new = jnp.maximum(m_old, row_max(cols))
            alpha = jnp.exp2(m_old - m_new)
            ps = [jnp.exp2(x - m_new) for x in cols]
            for c, pc in enumerate(ps):
                p_scr[h][rs, lanes(c)] = pc.astype(BF16)
            st[h][l_row, rs, :] = alpha * st[h][l_row, rs, :] + row_sum(ps)
            st[h][m_row, rs, :] = m_new
            st[h][a_row, rs, :] = alpha

    def softmax_once(h, width, mask):
        for r0 in range(0, rows, CH):
            rs = slice(r0, r0 + CH)
            cols = score_cols(h, r0, width, mask)
            m = row_max(cols)
            ps = [jnp.exp2(x - m) for x in cols]
            for c, pc in enumerate(ps):
                p_scr[h][rs, lanes(c)] = pc.astype(BF16)
            st[h][l_row, rs, :] = jnp.broadcast_to(row_sum(ps), (CH, LANES))

    def score_tile(h, kt):
        k0 = pl.multiple_of(kt * SEL_TK, SEL_TK)
        s_scr[h][:, :SEL_TK] = _dot_nt(qa[h][...], ke_ref[pl.ds(k0, SEL_TK), :])

    def value_tile(h, kt, live=None):
        k0 = pl.multiple_of(kt * SEL_TK, SEL_TK)
        pv = _dot(p_scr[h][:, :SEL_TK], vsel_ref[pl.ds(k0, SEL_TK), :])
        if live is not None:
            pv = jnp.where(live, pv, 0.0)
        acc[h][...] = st[h][a_row] * acc[h][...] + pv

    first = (lane & (HEAD_DIM // 2)) == 0
    for h in range(B_KV_HEADS):
        for g in range(B_GQA):
            c, par = (h * B_GQA + g) // 2, g % 2
            chunk = q_ref[:, lanes(c)].astype(F32)
            if par != h:
                chunk = pltpu.roll(chunk, HEAD_DIM // 2 if par == 0 else LANES - HEAD_DIM // 2, 1)
            chunk = jnp.where(first if h == 0 else jnp.logical_not(first), chunk, 0.0)
            qa[h][g * TQ:(g + 1) * TQ, :LANES] = chunk.astype(BF16)
        s_scr[h][:, :n_ck] = _dot_nt(qa[h][:, :LANES], kc_ref[...])

    for h in range(B_KV_HEADS):
        softmax_cmp(h)
        ot[h][...] = gate_col(0, h) * _dot(p_scr[h][:, :n_ck], vc_ref[...])

        p_slc = _dot_nt(wslct_ref[...], _split(imp_ref[h]))
        score = jnp.where(forced, FORCE_SCORE, jnp.where(causal_blk, p_slc, -1.0))
        for _ in range(N_SELECT):
            best = jnp.max(score, axis=0, keepdims=True)
            first = jnp.min(jnp.where(score == best, blk_f, float(LANES)), axis=0, keepdims=True)
            score = jnp.where(blk_f == first, -2.0, score)
        bias = jnp.where(score == -2.0, 0.0, NEG).T.astype(BF16)
        qa[h][:, LANES:] = jnp.concatenate([bias] * B_GQA, axis=0)

        st[h][m_row] = jnp.full((rows, LANES), NEG, F32)
        st[h][l_row] = jnp.zeros((rows, LANES), F32)
        st[h][a_row] = jnp.ones((rows, LANES), F32)
        acc[h][...] = jnp.zeros((rows, LANES), F32)

    score_tile(0, 0)

    def sweep(kt, carry):
        value_tile(1, jnp.maximum(kt - 1, 0), live=kt > 0)
        score_tile(1, kt)
        softmax_online(0, None)
        value_tile(0, kt)
        score_tile(0, kt + 1)
        softmax_online(1, None)
        return carry

    lax.fori_loop(0, n_full, sweep, 0)
    value_tile(1, jnp.maximum(n_full - 1, 0), live=n_full > 0)
    score_tile(1, n_full)
    softmax_online(0, 1)
    value_tile(0, n_full)
    softmax_online(1, 1)
    value_tile(1, n_full)

    kw = kwin_ref[pl.ds(w0, span), :]
    vw = vwin_ref[pl.ds(w0, span), :]
    for h in range(B_KV_HEADS):
        acc[h][...] = acc[h][...] / st[h][l_row]
        s_scr[h][:, :span] = _dot_nt(qa[h][:, :LANES], kw)
    for h in range(B_KV_HEADS):
        softmax_once(h, span, 2)
        o_win = _dot(p_scr[h][:, :span], vw) / st[h][l_row]
        o_tot = ot[h][...] + gate_col(1, h) * acc[h][...] + gate_col(2, h) * o_win

        for c2 in range(B_GQA // 2):
            halves = []
            for half in range(2):
                g = 2 * c2 + half
                piece = o_tot[g * TQ:(g + 1) * TQ]
                halves.append(piece if half == h else pltpu.roll(piece, HEAD_DIM, 1))
            o_ref[:, lanes(h * (B_GQA // 2) + c2)] = jnp.where(low, halves[0], halves[1]).astype(o_ref.dtype)


def _nsa(bqk, ke, bv, cmp_kv, fgate, wslct2, batch, seq, gate_block):
    t = bqk.shape[0]
    nq = seq // TQ
    n_ck = cmp_kv.shape[1] // batch
    n_cmp = seq // CMP_STRIDE - CMP_LEN // CMP_STRIDE + 1
    rows = B_GQA * TQ
    width = max(WIN + TQ, SEL_TK, n_ck)
    return pl.pallas_call(
        functools.partial(_nsa_kernel, n_cmp=n_cmp),
        grid=(batch, nq),
        in_specs=[
            pl.BlockSpec((TQ, B_QD), lambda b, i: (b * nq + i, 0)),
            pl.BlockSpec((seq, 2 * LANES), lambda b, i: (b, 0)),
            pl.BlockSpec((seq, LANES), lambda b, i: (b, 0)),
            pl.BlockSpec((seq, LANES), lambda b, i: (b, B_QD // LANES + 1)),
            pl.BlockSpec((seq, LANES), lambda b, i: (b, 1)),
            pl.BlockSpec((None, n_ck, LANES), lambda b, i: (0, b, 0)),
            pl.BlockSpec((None, n_ck, LANES), lambda b, i: (1, b, 0)),
            pl.BlockSpec((TQ, LANES), lambda b, i: (b * nq + i, gate_block)),
            pl.BlockSpec(wslct2.shape, lambda b, i: (0, 0)),
        ],
        out_specs=pl.BlockSpec((TQ, B_OUT), lambda b, i: (b * nq + i, 0)),
        out_shape=jax.ShapeDtypeStruct((t, B_OUT), BF16),
        scratch_shapes=(
            [pltpu.VMEM((rows, 2 * LANES), BF16)] * 2 +
            [pltpu.VMEM((rows, width), F32)] * 2 +
            [pltpu.VMEM((rows, width), BF16)] * 2 +
            [pltpu.VMEM((3, rows, LANES), F32)] * 2 +
            [pltpu.VMEM((rows, LANES), F32)] * 4 +
            [pltpu.VMEM((3, TQ, width), F32), pltpu.VMEM((TQ, n_ck), F32), pltpu.VMEM((2, TQ, n_ck), F32)]),
        compiler_params=_params("parallel", "parallel"),
        name="nsa_mixer",
    )(bqk, ke, bv, bqk, bv, cmp_kv, cmp_kv, fgate, wslct2)


def _merge_kernel(o0_ref, o1_ref, o2_ref, l0_ref, l1_ref, l2_ref, ob_ref, m0_ref, m1_ref, x_ref,
                  wpa_ref, wpb_ref, wout_ref, exp_ref, g_ref, xo_ref, h_ref, osc_ref, lsc_ref, oa_ref, mix_ref):
    def token_order(ref, scr, slot):
        d, n, width = ref.shape
        if d == 1:
            return ref[0]
        chunks = width // LANES
        for c in range(chunks):
            for r in range(d):
                scr[slot * chunks + c, pl.ds(r, n, stride=d), :] = ref[r, :, c * LANES:(c + 1) * LANES]
        return jnp.concatenate([scr[slot * chunks + c] for c in range(chunks)], axis=1)

    l0, l1, l2 = token_order(l0_ref, lsc_ref, 0), token_order(l1_ref, lsc_ref, 0), token_order(l2_ref, lsc_ref, 1)
    o0, o1, o2 = token_order(o0_ref, osc_ref, 0), token_order(o1_ref, osc_ref, 0), token_order(o2_ref, osc_ref, 1)
    mx = jnp.maximum(jnp.maximum(l0, l1), l2)
    e0, e1, e2 = jnp.exp2(l0 - mx), jnp.exp2(l1 - mx), jnp.exp2(l2 - mx)
    den = e0 + e1 + e2
    ex = exp_ref[...]
    out_a = _split_dot(e0 / den, ex) * o0 + _split_dot(e1 / den, ex) * o1 + _split_dot(e2 / den, ex) * o2
    oa_ref[...] = out_a.astype(BF16)
    d = x_ref.shape[1]
    for c0 in range(0, d, MXU_COLS):
        sl = slice(c0, c0 + MXU_COLS)
        pa = _dot(oa_ref[...], wpa_ref[:, sl])
        pb = _dot(ob_ref[...], wpb_ref[:, sl])
        mixed = jax.nn.sigmoid(m0_ref[:, sl]) * pa + jax.nn.sigmoid(m1_ref[:, sl]) * pb
        mix_ref[:, sl] = mixed.astype(BF16)
    ssq = None
    for c0 in range(0, d, MXU_COLS):
        sl = slice(c0, c0 + MXU_COLS)
        x = x_ref[:, sl] + _dot(mix_ref[...], wout_ref[:, sl])
        xo_ref[:, sl] = x
        part = jnp.sum(x * x, axis=-1, keepdims=True)
        ssq = part if ssq is None else ssq + part
    scale = lax.rsqrt(ssq * (1.0 / d) + EPS)
    for c0 in range(0, d, MXU_COLS):
        sl = slice(c0, c0 + MXU_COLS)
        h_ref[:, sl] = (xo_ref[:, sl] * scale * g_ref[:, sl]).astype(h_ref.dtype)


def _merge(o_groups, lse_groups, out_b, fgate, x, wpa, wpb, wout, expand2, g_next, seq, tm=512):
    t, d = x.shape
    nseq = seq // tm
    row = lambda w: pl.BlockSpec((tm, w), lambda i: (i, 0))
    full = lambda a: pl.BlockSpec(a.shape, lambda i: (0,) * a.ndim)
    grouped = lambda a: pl.BlockSpec((None, a.shape[1], tm // a.shape[1], a.shape[3]),
                                     lambda i: (i // nseq, 0, i % nseq, 0))
    g_next = g_next.reshape(1, d)
    return pl.pallas_call(
        _merge_kernel,
        grid=(t // tm,),
        in_specs=[grouped(a) for a in o_groups] + [grouped(a) for a in lse_groups] + [
            row(B_OUT), pl.BlockSpec((tm, d), lambda i: (i, 0)), pl.BlockSpec((tm, d), lambda i: (i, 1)), row(d),
            full(wpa), full(wpb), full(wout), full(expand2), full(g_next)],
        out_specs=[row(d), row(d)],
        out_shape=[jax.ShapeDtypeStruct((t, d), F32), jax.ShapeDtypeStruct((t, d), BF16)],
        scratch_shapes=[pltpu.VMEM((2 * A_OUT // LANES, tm, LANES), F32), pltpu.VMEM((2, tm, LANES), F32),
                        pltpu.VMEM((tm, A_OUT), BF16), pltpu.VMEM((tm, d), BF16)],
        compiler_params=_params("parallel"),
        name="merge_out",
    )(*o_groups, *lse_groups, out_b, fgate, fgate, x, wpa, wpb, wout, expand2, g_next)


def _ffn_up_kernel(h_ref, wg_ref, wu_ref, cw_ref, cb_ref, o_ref, gbuf_ref, *, tiles_per_seq):
    i = pl.program_id(1)
    tm = h_ref.shape[0]
    h = h_ref[...]

    @pl.when(i % tiles_per_seq == 0)
    def _():
        gbuf_ref[0:8, :] = jnp.zeros((8, gbuf_ref.shape[1]), F32)

    @pl.when(i % tiles_per_seq != 0)
    def _():
        gbuf_ref[0:8, :] = gbuf_ref[tm:tm + 8, :]

    gbuf_ref[8:tm + 8, :] = _dot(h, wg_ref[...])
    u = _dot(h, wu_ref[...])
    conv = cb_ref[...] + cw_ref[0:1, :] * gbuf_ref[pl.ds(6, tm), :]
    conv = conv + cw_ref[1:2, :] * gbuf_ref[pl.ds(7, tm), :]
    conv = conv + cw_ref[2:3, :] * gbuf_ref[pl.ds(8, tm), :]
    o_ref[...] = (jax.nn.silu(conv) * u).astype(o_ref.dtype)


def _ffn_up(h, w_up, conv_w, conv_b, seq, tm=512, tn=1408):
    t, d = h.shape
    nj = D_FF // tn
    return pl.pallas_call(
        functools.partial(_ffn_up_kernel, tiles_per_seq=seq // tm),
        grid=(nj, t // tm),
        in_specs=[
            pl.BlockSpec((tm, d), lambda j, i: (i, 0)),
            pl.BlockSpec((d, tn), lambda j, i: (0, j)),
            pl.BlockSpec((d, tn), lambda j, i: (0, nj + j)),
            pl.BlockSpec((CONV_W, tn), lambda j, i: (0, j)),
            pl.BlockSpec((1, tn), lambda j, i: (0, j)),
        ],
        out_specs=pl.BlockSpec((tm, tn), lambda j, i: (i, j)),
        out_shape=jax.ShapeDtypeStruct((t, D_FF), BF16),
        scratch_shapes=[pltpu.VMEM((tm + 8, tn), F32)],
        compiler_params=_params("arbitrary", "arbitrary"),
        name="ffn_up_conv",
    )(h, w_up, w_up, conv_w, conv_b.reshape(1, D_FF))


def _ffn_down_kernel(a_ref, w_ref, x_ref, g_ref, xo_ref, h_ref):
    x = x_ref[...] + _dot(a_ref[...], w_ref[...])
    xo_ref[...] = x
    y = x * lax.rsqrt(jnp.mean(x * x, axis=-1, keepdims=True) + EPS)
    h_ref[...] = (y * g_ref[...]).astype(h_ref.dtype)


def _ffn_down(act, w_down, x, g_next, tm=512):
    t, d = x.shape
    row = lambda w: pl.BlockSpec((tm, w), lambda i: (i, 0))
    return pl.pallas_call(
        _ffn_down_kernel,
        grid=(t // tm,),
        in_specs=[row(D_FF), pl.BlockSpec(w_down.shape, lambda i: (0, 0)), row(d),
                  pl.BlockSpec((1, d), lambda i: (0, 0))],
        out_specs=[row(d), row(d)],
        out_shape=[jax.ShapeDtypeStruct((t, d), F32), jax.ShapeDtypeStruct((t, d), BF16)],
        compiler_params=_params("parallel"),
        name="ffn_down",
    )(act, w_down, x, g_next.reshape(1, d))


def _rope_tables(pos):
    half = HEAD_DIM // 2
    inv_freq = ROPE_THETA ** (-jnp.arange(half, dtype=F32) / half)
    ang = pos.astype(F32)[:, None] * inv_freq[None, :]
    c, s = jnp.cos(ang), jnp.sin(ang)
    return jnp.tile(c, (1, 4)), jnp.concatenate([-s, -s, s, s], axis=1)


def _pair_order(n):
    half = HEAD_DIM // 2
    pair = np.concatenate([np.arange(0, half), np.arange(HEAD_DIM, HEAD_DIM + half),
                           np.arange(half, HEAD_DIM), np.arange(HEAD_DIM + half, LANES)])
    cols = np.arange(n)
    return (cols // LANES) * LANES + pair[cols % LANES]


def _pair_gain(g):
    half = HEAD_DIM // 2
    return jnp.concatenate([g[:half], g[:half], g[half:], g[half:]])


def _constants(n_ck, seq):
    lane = np.arange(LANES)
    head = (lane // (HEAD_DIM // 2)) % 2
    seg = (head[:, None] == head[None, :]).astype(np.float32) / HEAD_DIM
    ratio = SEL_LEN // CMP_STRIDE
    wslc = np.zeros((n_ck, LANES), np.float32)
    for off in range(CMP_LEN // CMP_STRIDE):
        for i in range(n_ck):
            if i + off < n_ck:
                wslc[i, (i + off) // ratio] += 1.0
    wslct2 = np.concatenate([wslc, wslc], axis=0).T
    expand = np.zeros((LANES, A_OUT), np.float32)
    for j in range(A_HEADS_PER_GROUP):
        expand[j, j * HEAD_DIM:(j + 1) * HEAD_DIM] = 1.0
    expand2 = np.concatenate([expand, expand], axis=0)
    onehot = (np.arange(seq)[:, None] // SEL_LEN == lane[None, :]).astype(np.float32)
    return (jnp.asarray(seg, BF16), jnp.asarray(wslct2, BF16), jnp.asarray(expand2, BF16),
            jnp.asarray(onehot, BF16))


def kernel(x, norm_mix_g, w_in, a_q_g, a_k_g, b_q_g, b_k_g, cmp_pos, cmp_w1, cmp_w2, w_proj_a, w_proj_b,
           w_out, norm_ffn_g, w_up, conv_w, conv_b, w_down):
    batch, seq, d = x.shape
    depth = w_in.shape[0]
    t = batch * seq
    assert d == D_MODEL and seq % (16 * TQ) == 0 and seq // SEL_LEN <= LANES
    n_ck = seq // CMP_STRIDE
    seg, wslct2, expand2, onehot = _constants(n_ck, seq)
    onehot = jnp.tile(onehot, (batch, 1))
    cos_t, sin_t = _rope_tables(jnp.arange(seq))
    cos_c, sin_c = _rope_tables(jnp.arange(n_ck) * CMP_STRIDE + CMP_LEN - 1)
    eye_h = jnp.eye(B_KV_HEADS, dtype=F32)
    o_kq, o_kv, o_gate, o_merge = A_QKV, A_QKV + B_QD, A_QKV + B_QD + B_KV, A_QKV + B_QD + B_KV + B_GATE
    n_a = A_HEADS * HEAD_DIM
    f_gate_block = 2 * D_MODEL // LANES

    xf = x.reshape(t, d)
    h = _rmsnorm(xf, norm_mix_g[0])
    for l in range(depth):
        w = w_in[l]
        kv = lambda i: w[:, o_kv + i * LANES:o_kv + (i + 1) * LANES]
        w_bqk = jnp.concatenate([w[:, o_kq:o_kv], kv(2), kv(4)], axis=1)[:, _pair_order(B_QD + 2 * LANES)]
        w_bqk = w_bqk.astype(BF16)
        w_bv = jnp.concatenate([kv(3), kv(5)], axis=1).astype(BF16)
        w_f = jnp.concatenate([w[:, o_merge:], w[:, o_gate:o_merge],
                               jnp.zeros((d, LANES - B_GATE), F32), kv(0), kv(1)], axis=1).astype(BF16)
        gain_a = jnp.concatenate([jnp.tile(_pair_gain(a_q_g[l] * Q_SCALE), A_HEADS_PER_GROUP // 2),
                                  jnp.tile(_pair_gain(a_k_g[l]), A_HEADS_PER_GROUP // 2)]).reshape(1, -1)
        gain_b = jnp.concatenate([jnp.tile(_pair_gain(b_q_g[l] * Q_SCALE), B_Q_HEADS // 2),
                                  _pair_gain(b_k_g[l, 1]), _pair_gain(b_k_g[l, 2])]).reshape(1, -1)

        o_groups, lse_groups = [], []
        for gi, (_, dil) in enumerate(DIL_PAIRS):
            cols = [w[:, part * n_a + gi * A_OUT:part * n_a + (gi + 1) * A_OUT] for part in range(3)]
            cols = [cols[0][:, _pair_order(A_OUT)], cols[1][:, _pair_order(A_OUT)], cols[2]]
            qkv = _proj_group(h, jnp.concatenate(cols, axis=1).astype(BF16), gain_a, cos_t, sin_t, seg,
                              dil, batch, seq)
            o_g, lse_g = _band_attention(qkv, dil, batch, seq)
            o_groups.append(o_g)
            lse_groups.append(lse_g)

        bqk = _proj(h, w_bqk, BF16, tn=768, rope=(gain_b, cos_t, sin_t, seg), seq=seq)
        bv = _proj(h, w_bv, BF16, tn=256)
        fg = _proj(h, w_f, F32, tn=w_f.shape[1], tm=256)
        pe = jnp.tile(cmp_pos[l], (1, 1, B_KV_HEADS))
        w1 = cmp_w1[l].reshape(2, CMP_LEN, HEAD_DIM, CMP_HIDDEN)
        w1e = jnp.einsum('kpdn,hj->kphdjn', w1, eye_h).reshape(2, CMP_LEN, LANES, 2 * CMP_HIDDEN)
        w2e = jnp.einsum('knd,hj->khnjd', cmp_w2[l], eye_h).reshape(2, 2 * CMP_HIDDEN, LANES)
        w2e = jnp.stack([w2e[0][:, _pair_order(LANES)], w2e[1]])
        gain_c = _pair_gain(b_k_g[l, 0]).reshape(1, LANES)
        cmp_kv = _compress(fg, f_gate_block + 1, pe, w1e.astype(BF16), w2e.astype(BF16), gain_c,
                           cos_c, sin_c, seg, batch, seq)
        ke = jnp.concatenate([bqk[:, B_QD:B_QD + LANES], onehot], axis=1)
        out_b = _nsa(bqk, ke, bv, cmp_kv, fg, wslct2, batch, seq, f_gate_block)

        xf, h = _merge(o_groups, lse_groups, out_b, fg, xf, w_proj_a[l].astype(BF16), w_proj_b[l].astype(BF16),
                       w_out[l].astype(BF16), expand2, norm_ffn_g[l], seq)
        act = _ffn_up(h, w_up[l].astype(BF16), conv_w[l], conv_b[l], seq)
        g_next = norm_mix_g[l + 1] if l + 1 < depth else jnp.ones((d,), F32)
        xf, h = _ffn_down(act, w_down[l].astype(BF16), xf, g_next)
    return xf.reshape(batch, seq, d)
```
